```python
import math
import jax, jax.numpy as jnp
from jax import lax
import numpy as np

D_MODEL = 1024
BATCH = 4
SEQ = 4096
DEPTH = 2

GRID_W = 64
CTX_LEN = 256
N_MIXERS = 2
Q_BLOCK = 128
NORM_EPS = 1e-6
ROPE_THETA = 10000.0
N_MOD = 6

MLA_HEADS = 8
MLA_Q_RANK = 384
MLA_KV_RANK = 256
MLA_NOPE_DIM = 128
MLA_ROPE_DIM = 64
MLA_V_DIM = 128

GQA_HEADS = 8
GQA_KV_HEADS = 2
GQA_GROUP = GQA_HEADS // GQA_KV_HEADS
GQA_HEAD_DIM = 128

PEER_HEADS = 8
PEER_N_KEYS = 128
PEER_N_EXPERTS = PEER_N_KEYS * PEER_N_KEYS
PEER_KEY_DIM = 256
PEER_HALF_DIM = PEER_KEY_DIM // 2
PEER_TOPK = 16
PEER_CHUNK = 128

kernel_name = 'hybrid_mla_gqa_peer_dit'


def rmsnorm(x, g):
    xf = x.astype(jnp.float32)
    y = xf * lax.rsqrt(jnp.mean(xf * xf, axis=-1, keepdims=True) + NORM_EPS)
    return (y * g.astype(jnp.float32)).astype(x.dtype)


def modulate(h, shift, scale):
    return h * (1 + scale) + shift


def axial_rope_tables(n_tokens, rot_dim, dtype):
    rows = n_tokens // GRID_W
    row = jnp.repeat(jnp.arange(rows, dtype=jnp.float32), GRID_W)
    col = jnp.tile(jnp.arange(GRID_W, dtype=jnp.float32), rows)
    d_axis = rot_dim // 2
    inv_freq = ROPE_THETA ** (-jnp.arange(0, d_axis, 2, dtype=jnp.float32) / d_axis)
    ang_r = row[:, None] * inv_freq
    ang_c = col[:, None] * inv_freq
    ang = jnp.concatenate([ang_r, ang_r, ang_c, ang_c], axis=-1)
    return jnp.cos(ang).astype(dtype), jnp.sin(ang).astype(dtype)


def apply_axial_rope(x, cos, sin):
    a1, a2, b1, b2 = jnp.split(x, 4, axis=-1)
    rot = jnp.concatenate([-a2, a1, -b2, b1], axis=-1)
    return x * cos + rot * sin


def sweep_query_blocks(fn, *qs):
    b, n = qs[0].shape[:2]
    nb = n // Q_BLOCK
    blocks = tuple(jnp.swapaxes(q.reshape((b, nb, Q_BLOCK) + q.shape[2:]), 0, 1) for q in qs)
    out = lax.map(lambda args: fn(*args), blocks)
    out = jnp.swapaxes(out, 0, 1)
    return out.reshape((b, n) + out.shape[3:])


def mla_project(h, w_in, q_norm, w_uq, kv_norm, w_ukv):
    b, n, _ = h.shape
    a = h @ w_in
    cq, ckv, k_rope = jnp.split(a, [MLA_Q_RANK, MLA_Q_RANK + MLA_KV_RANK], axis=-1)
    q = (rmsnorm(cq, q_norm) @ w_uq).reshape(b, n, MLA_HEADS, MLA_NOPE_DIM + MLA_ROPE_DIM)
    kv = (rmsnorm(ckv, kv_norm) @ w_ukv).reshape(b, n, MLA_HEADS, MLA_NOPE_DIM + MLA_V_DIM)
    q_nope, q_rope = jnp.split(q, [MLA_NOPE_DIM], axis=-1)
    k_nope, v = jnp.split(kv, [MLA_NOPE_DIM], axis=-1)
    return q_nope, q_rope, k_nope, k_rope, v


def mla_attend(q_n, q_r, k_n, k_r, v):
    scale = 1.0 / math.sqrt(MLA_NOPE_DIM + MLA_ROPE_DIM)
    s = (jnp.einsum('bqhd,bkhd->bhqk', q_n, k_n, preferred_element_type=jnp.float32)
         + jnp.einsum('bqhd,bkd->bhqk', q_r, k_r, preferred_element_type=jnp.float32))
    p = jax.nn.softmax(s * scale, axis=-1).astype(v.dtype)
    return jnp.einsum('bhqk,bkhd->bqhd', p, v)


def mla_mixer(h, hc, w_in, q_norm, w_uq, kv_norm, w_ukv, w_o, need_ctx):
    b, n, _ = h.shape
    cos, sin = axial_rope_tables(n, MLA_ROPE_DIM, h.dtype)
    qn, qr, kn, kr, v = mla_project(h, w_in, q_norm, w_uq, kv_norm, w_ukv)
    qr = apply_axial_rope(qr, cos[:, None, :], sin[:, None, :])
    kr = apply_axial_rope(kr, cos, sin)
    qn_c, qr_c, kn_c, kr_c, v_c = mla_project(hc, w_in, q_norm, w_uq, kv_norm, w_ukv)
    kn_all = jnp.concatenate([kn_c, kn], axis=1)
    kr_all = jnp.concatenate([kr_c, kr], axis=1)
    v_all = jnp.concatenate([v_c, v], axis=1)
    y = sweep_query_blocks(lambda a, r: mla_attend(a, r, kn_all, kr_all, v_all), qn, qr)
    y = y.reshape(b, n, MLA_HEADS * MLA_V_DIM) @ w_o
    y_c = None
    if need_ctx:
        y_c = mla_attend(qn_c, qr_c, kn_c, kr_c, v_c).reshape(b, hc.shape[1], MLA_HEADS * MLA_V_DIM) @ w_o
    return y, y_c


def gqa_project(h, w_qkv, q_norm, k_norm):
    b, n, _ = h.shape
    q, k, v = jnp.split(h @ w_qkv, [GQA_HEADS * GQA_HEAD_DIM, (GQA_HEADS + GQA_KV_HEADS) * GQA_HEAD_DIM], axis=-1)
    q = rmsnorm(q.reshape(b, n, GQA_KV_HEADS, GQA_GROUP, GQA_HEAD_DIM), q_norm)
    k = rmsnorm(k.reshape(b, n, GQA_KV_HEADS, GQA_HEAD_DIM), k_norm)
    v = v.reshape(b, n, GQA_KV_HEADS, GQA_HEAD_DIM)
    return q, k, v


def gqa_attend(q, k, v):
    scale = 1.0 / math.sqrt(GQA_HEAD_DIM)
    s = jnp.einsum('bqgrd,bkgd->bgrqk', q, k, preferred_element_type=jnp.float32)
    p = jax.nn.softmax(s * scale, axis=-1).astype(v.dtype)
    return jnp.einsum('bgrqk,bkgd->bqgrd', p, v)


def gqa_mixer(h, hc, w_qkv, q_norm, k_norm, w_o, need_ctx):
    b, n, _ = h.shape
    cos, sin = axial_rope_tables(n, GQA_HEAD_DIM, h.dtype)
    q, k, v = gqa_project(h, w_qkv, q_norm, k_norm)
    q = apply_axial_rope(q, cos[:, None, None, :], sin[:, None, None, :])
    k = apply_axial_rope(k, cos[:, None, :], sin[:, None, :])
    q_c, k_c, v_c = gqa_project(hc, w_qkv, q_norm, k_norm)
    k_all = jnp.concatenate([k_c, k], axis=1)
    v_all = jnp.concatenate([v_c, v], axis=1)
    y = sweep_query_blocks(lambda qb: gqa_attend(qb, k_all, v_all), q)
    y = y.reshape(b, n, GQA_HEADS * GQA_HEAD_DIM) @ w_o
    y_c = None
    if need_ctx:
        y_c = gqa_attend(q_c, k_c, v_c).reshape(b, hc.shape[1], GQA_HEADS * GQA_HEAD_DIM) @ w_o
    return y, y_c


def peer_ffn(h, w_query, sub_k1, sub_k2, u_tab, v_tab):
    b, n, d = h.shape
    nt = b * n
    t = h.reshape(nt, d)
    q = (t @ w_query).reshape(nt, PEER_HEADS, 2, PEER_HALF_DIM)
    s1 = jnp.einsum('thd,hkd->thk', q[:, :, 0], sub_k1, preferred_element_type=jnp.float32)
    s2 = jnp.einsum('thd,hkd->thk', q[:, :, 1], sub_k2, preferred_element_type=jnp.float32)
    v1, i1 = lax.top_k(s1, PEER_TOPK)
    v2, i2 = lax.top_k(s2, PEER_TOPK)
    cand = (v1[..., :, None] + v2[..., None, :]).reshape(nt, PEER_HEADS, PEER_TOPK * PEER_TOPK)
    cand_idx = (i1[..., :, None] * PEER_N_KEYS + i2[..., None, :]).reshape(nt, PEER_HEADS, PEER_TOPK * PEER_TOPK)
    top_s, top_pos = lax.top_k(cand, PEER_TOPK)
    expert_idx = jnp.take_along_axis(cand_idx, top_pos, axis=-1)
    gates = jax.nn.softmax(top_s, axis=-1).astype(h.dtype)

    def chunk(args):
        tc, ec, gc = args
        a = jax.nn.gelu(jnp.einsum('td,thkd->thk', tc, u_tab[ec]), approximate=False)
        return jnp.einsum('thk,thkd->td', gc * a, v_tab[ec])

    nc = nt // PEER_CHUNK
    out = lax.map(chunk, (t.reshape(nc, PEER_CHUNK, d),
                          expert_idx.reshape(nc, PEER_CHUNK, PEER_HEADS, PEER_TOPK),
                          gates.reshape(nc, PEER_CHUNK, PEER_HEADS, PEER_TOPK)))
    return out.reshape(b, n, d)


def setup_inputs(seed: int = 0) -> dict:
    key = jax.random.key(seed)
    ks = jax.random.split(key, 64)
    counter = [0]

    def nrm(shape, scale):
        k = ks[counter[0]]
        counter[0] += 1
        return jax.random.normal(k, shape, jnp.float32) * scale

    def gain(n):
        return 1.0 + nrm((n,), 0.02)

    d = D_MODEL
    inp = {
        'x': nrm((BATCH, SEQ, d), 1.0),
        'c': nrm((BATCH, d), 1.0),
        'ctx': nrm((BATCH, CTX_LEN, d), 1.0),
        'c_ctx': nrm((d,), 1.0),
    }
    for i in range(DEPTH):
        pre = 'l%d_' % i
        inp[pre + 'ada_w'] = nrm((d, N_MOD * d), 0.5 * d ** -0.5)
        inp[pre + 'ada_b'] = nrm((N_MOD * d,), 0.01)
        inp[pre + 'norm_mix'] = gain(d)
        inp[pre + 'norm_ffn'] = gain(d)
        if i % N_MIXERS == 0:
            inp[pre + 'mla_w_in'] = nrm((d, MLA_Q_RANK + MLA_KV_RANK + MLA_ROPE_DIM), d ** -0.5)
            inp[pre + 'mla_q_norm'] = gain(MLA_Q_RANK)
            inp[pre + 'mla_w_uq'] = nrm((MLA_Q_RANK, MLA_HEADS * (MLA_NOPE_DIM + MLA_ROPE_DIM)), MLA_Q_RANK ** -0.5)
            inp[pre + 'mla_kv_norm'] = gain(MLA_KV_RANK)
            inp[pre + 'mla_w_ukv'] = nrm((MLA_KV_RANK, MLA_HEADS * (MLA_NOPE_DIM + MLA_V_DIM)), MLA_KV_RANK ** -0.5)
            inp[pre + 'mla_w_o'] = nrm((MLA_HEADS * MLA_V_DIM, d), (MLA_HEADS * MLA_V_DIM) ** -0.5)
        else:
            inp[pre + 'gqa_w_qkv'] = nrm((d, (GQA_HEADS + 2 * GQA_KV_HEADS) * GQA_HEAD_DIM), d ** -0.5)
            inp[pre + 'gqa_q_norm'] = gain(GQA_HEAD_DIM)
            inp[pre + 'gqa_k_norm'] = gain(GQA_HEAD_DIM)
            inp[pre + 'gqa_w_o'] = nrm((GQA_HEADS * GQA_HEAD_DIM, d), (GQA_HEADS * GQA_HEAD_DIM) ** -0.5)
        inp[pre + 'peer_w_query'] = nrm((d, PEER_HEADS * PEER_KEY_DIM), d ** -0.5)
        inp[pre + 'peer_k1'] = nrm((PEER_HEADS, PEER_N_KEYS, PEER_HALF_DIM), PEER_HALF_DIM ** -0.5)
        inp[pre + 'peer_k2'] = nrm((PEER_HEADS, PEER_N_KEYS, PEER_HALF_DIM), PEER_HALF_DIM ** -0.5)
        inp[pre + 'peer_u'] = nrm((PEER_N_EXPERTS, d), d ** -0.5)
        inp[pre + 'peer_v'] = nrm((PEER_N_EXPERTS, d), 1.0)
    inp['norm_out'] = gain(d)
    return inp


def reference(x, c, ctx, c_ctx,
              l0_ada_w, l0_ada_b, l0_norm_mix, l0_norm_ffn,
              l0_mla_w_in, l0_mla_q_norm, l0_mla_w_uq, l0_mla_kv_norm, l0_mla_w_ukv, l0_mla_w_o,
              l0_peer_w_query, l0_peer_k1, l0_peer_k2, l0_peer_u, l0_peer_v,
              l1_ada_w, l1_ada_b, l1_norm_mix, l1_norm_ffn,
              l1_gqa_w_qkv, l1_gqa_q_norm, l1_gqa_k_norm, l1_gqa_w_o,
              l1_peer_w_query, l1_peer_k1, l1_peer_k2, l1_peer_u, l1_peer_v,
              norm_out):
    layers = [
        dict(ada_w=l0_ada_w, ada_b=l0_ada_b, norm_mix=l0_norm_mix, norm_ffn=l0_norm_ffn,
             mixer=(l0_mla_w_in, l0_mla_q_norm, l0_mla_w_uq, l0_mla_kv_norm, l0_mla_w_ukv, l0_mla_w_o),
             peer=(l0_peer_w_query, l0_peer_k1, l0_peer_k2, l0_peer_u, l0_peer_v)),
        dict(ada_w=l1_ada_w, ada_b=l1_ada_b, norm_mix=l1_norm_mix, norm_ffn=l1_norm_ffn,
             mixer=(l1_gqa_w_qkv, l1_gqa_q_norm, l1_gqa_k_norm, l1_gqa_w_o),
             peer=(l1_peer_w_query, l1_peer_k1, l1_peer_k2, l1_peer_u, l1_peer_v)),
    ]
    mixers = (mla_mixer, gqa_mixer)
    xc = ctx
    for i in range(DEPTH):
        p = layers[i]
        need_ctx = i < DEPTH - 1
        mod = jax.nn.silu(c) @ p['ada_w'] + p['ada_b']
        mod_c = jax.nn.silu(c_ctx) @ p['ada_w'] + p['ada_b']
        sh_a, sc_a, g_a, sh_f, sc_f, g_f = [m[:, None, :] for m in jnp.split(mod, N_MOD, axis=-1)]
        csh_a, csc_a, cg_a, csh_f, csc_f, cg_f = jnp.split(mod_c, N_MOD, axis=-1)
        h = modulate(rmsnorm(x, p['norm_mix']), sh_a, sc_a)
        hc = modulate(rmsnorm(xc, p['norm_mix']), csh_a, csc_a)
        y, y_c = mixers[i % N_MIXERS](h, hc, *p['mixer'], need_ctx=need_ctx)
        x = x + g_a * y
        h = modulate(rmsnorm(x, p['norm_ffn']), sh_f, sc_f)
        x = x + g_f * peer_ffn(h, *p['peer'])
        if need_ctx:
            xc = xc + cg_a * y_c
            hc = modulate(rmsnorm(xc, p['norm_ffn']), csh_f, csc_f)
            xc = xc + cg_f * peer_ffn(hc, *p['peer'])
    return rmsnorm(x, norm_out)
```

```python
import functools
import math

import jax
import jax.numpy as jnp
import numpy as np
from jax import lax
from jax.experimental import pallas as pl
from jax.experimental.pallas import tpu as pltpu

F32 = jnp.float32
BF16 = jnp.bfloat16

GRID_W = 64
NORM_EPS = 1e-6
ROPE_THETA = 10000.0
N_MOD = 6
MLA_HEADS = 8
MLA_Q_RANK = 384
MLA_KV_RANK = 256
MLA_NOPE = 128
MLA_ROPE = 64
MLA_V = 128
GQA_HEADS = 8
GQA_KV_HEADS = 2
GQA_DIM = 128
PEER_HEADS = 8
PEER_KEYS = 128
PEER_HALF = 128
PEER_TOPK = 16

LANES = 128
SUBLANES = 8
VMEM_LIMIT = 56 * 1024 * 1024

ROW_TILE = 512
ATTN_TQ = 256
ROUTE_TL = 128
EXP_TM = 512
EXP_TE = 1024

_NT = (((1,), (1,)), ((), ()))


def _cparams(sem):
    return pltpu.CompilerParams(dimension_semantics=sem, vmem_limit_bytes=VMEM_LIMIT)


def _rmsn(x, g):
    ms = jnp.mean(x * x, axis=-1, keepdims=True)
    return x * lax.rsqrt(ms + NORM_EPS) * g


def _mod_kernel(c_ref, w_ref, b_ref, o_ref):
    s = jax.nn.silu(c_ref[...])
    o_ref[...] = jnp.dot(s, w_ref[...], preferred_element_type=F32,
                         precision=lax.Precision.HIGHEST) + b_ref[...]


def _modulation(cc, w, b):
    d, n = w.shape
    bn = n // 4
    return pl.pallas_call(
        _mod_kernel,
        grid=(n // bn,),
        in_specs=[pl.BlockSpec((SUBLANES, d), lambda j: (0, 0)),
                  pl.BlockSpec((d, bn), lambda j: (0, j)),
                  pl.BlockSpec((1, bn), lambda j: (0, j))],
        out_specs=pl.BlockSpec((SUBLANES, bn), lambda j: (0, j)),
        out_shape=jax.ShapeDtypeStruct((SUBLANES, n), F32),
        compiler_params=_cparams(("arbitrary",)),
        name="adaln_mod",
    )(cc, w, b.reshape(1, n))


def _rope_tables(seq, rot_dim, extra_rows):
    rows = seq // GRID_W
    row = jnp.repeat(jnp.arange(rows, dtype=F32), GRID_W)
    col = jnp.tile(jnp.arange(GRID_W, dtype=F32), rows)
    d_axis = rot_dim // 2
    inv_freq = ROPE_THETA ** (-jnp.arange(0, d_axis, 2, dtype=F32) / d_axis)
    ang_r = row[:, None] * inv_freq
    ang_c = col[:, None] * inv_freq
    ang = jnp.concatenate([ang_r, ang_r, ang_c, ang_c], axis=-1)
    cos, sin = jnp.cos(ang), jnp.sin(ang)
    pad = LANES - rot_dim
    cos = jnp.pad(cos, ((0, extra_rows), (0, pad)), constant_values=1.0)
    cos = cos.at[seq:, :].set(1.0)
    sin = jnp.pad(sin, ((0, extra_rows), (0, pad)))
    quarter = rot_dim // 4
    first = (jnp.arange(LANES) % (2 * quarter)) < quarter
    sin_a = jnp.where(first[None, :], -sin, 0.0)
    sin_b = jnp.where(first[None, :], 0.0, sin)
    return cos, sin_a, sin_b


def _rope(c, cos, sin_a, sin_b, quarter):
    return (c * cos + pltpu.roll(c, LANES - quarter, 1) * sin_a
            + pltpu.roll(c, quarter, 1) * sin_b)


def _mla_proj_kernel(x_ref, g_ref, mod_ref, win_ref, qn_ref, wuq_ref, kvn_ref, wukv_ref,
                     cos_ref, sa_ref, sb_ref, q_out, k_out, v_out):
    h = _rmsn(x_ref[...], g_ref[...])
    h = h * (1.0 + mod_ref[0, 1]) + mod_ref[0, 0]
    a = jnp.dot(h.astype(BF16), win_ref[...], preferred_element_type=F32)
    cq = _rmsn(a[:, :MLA_Q_RANK], qn_ref[...]).astype(BF16)
    ckv = _rmsn(a[:, MLA_Q_RANK:MLA_Q_RANK + MLA_KV_RANK], kvn_ref[...]).astype(BF16)
    cos, sa, sb = cos_ref[...], sa_ref[...], sb_ref[...]
    quarter = MLA_ROPE // 4
    kr = _rope(a[:, MLA_Q_RANK + MLA_KV_RANK:], cos, sa, sb, quarter).astype(BF16)
    q = jnp.dot(cq, wuq_ref[...], preferred_element_type=F32)
    kv = jnp.dot(ckv, wukv_ref[...], preferred_element_type=F32)
    scale = 1.0 / math.sqrt(MLA_NOPE + MLA_ROPE)
    hw = 2 * LANES
    for hd in range(MLA_HEADS):
        q_out[:, hd * hw:hd * hw + LANES] = (q[:, hd * hw:hd * hw + LANES] * scale).astype(BF16)
        qr = _rope(q[:, hd * hw + LANES:(hd + 1) * hw], cos, sa, sb, quarter) * scale
        q_out[:, hd * hw + LANES:(hd + 1) * hw] = qr.astype(BF16)
        k_out[:, hd * hw:hd * hw + LANES] = kv[:, hd * LANES:(hd + 1) * LANES].astype(BF16)
        k_out[:, hd * hw + LANES:(hd + 1) * hw] = kr
    v_out[...] = kv[:, MLA_HEADS * LANES:].astype(BF16)


def _gqa_proj_kernel(x_ref, g_ref, mod_ref, wqkv_ref, qn_ref, kn_ref,
                     cos_ref, sa_ref, sb_ref, q_out, k_out, v_out):
    h = _rmsn(x_ref[...], g_ref[...])
    h = h * (1.0 + mod_ref[0, 1]) + mod_ref[0, 0]
    qkv = jnp.dot(h.astype(BF16), wqkv_ref[...], preferred_element_type=F32)
    cos, sa, sb = cos_ref[...], sa_ref[...], sb_ref[...]
    quarter = GQA_DIM // 4
    scale = 1.0 / math.sqrt(GQA_DIM)
    for hd in range(GQA_HEADS):
        c = _rmsn(qkv[:, hd * LANES:(hd + 1) * LANES], qn_ref[...])
        q_out[:, hd * LANES:(hd + 1) * LANES] = (_rope(c, cos, sa, sb, quarter) * scale).astype(BF16)
    k0 = GQA_HEADS * LANES
    for hd in range(GQA_KV_HEADS):
        c = _rmsn(qkv[:, k0 + hd * LANES:k0 + (hd + 1) * LANES], kn_ref[...])
        k_out[:, hd * LANES:(hd + 1) * LANES] = _rope(c, cos, sa, sb, quarter).astype(BF16)
    v_out[...] = qkv[:, k0 + GQA_KV_HEADS * LANES:].astype(BF16)


def _row_maps(n_lat_tiles, tiles_per_batch, n_batch, tiles_per_seq):
    def row(i):
        return (i, 0)

    def const(i):
        return (0, 0)

    def mod(i):
        return (jnp.minimum(i // tiles_per_batch, n_batch), 0, 0, 0)

    def rope(i):
        return (jnp.where(i < n_lat_tiles, i % tiles_per_seq, tiles_per_seq), 0)

    return row, const, mod, rope


def _mixer_proj(kern, xa, gain, mod, weights, tables, out_widths, geom, name):
    n_rows, d = xa.shape
    tm = ROW_TILE
    row, const, modmap, ropemap = _row_maps(*geom)
    in_specs = [pl.BlockSpec((tm, d), row),
                pl.BlockSpec((1, d), const),
                pl.BlockSpec((1, N_MOD, 1, d), modmap)]
    in_specs += [pl.BlockSpec(w.shape, const) for w in weights]
    in_specs += [pl.BlockSpec((tm, LANES), ropemap) for _ in tables]
    return pl.pallas_call(
        kern,
        grid=(n_rows // tm,),
        in_specs=in_specs,
        out_specs=[pl.BlockSpec((tm, w), row) for w in out_widths],
        out_shape=[jax.ShapeDtypeStruct((n_rows, w), BF16) for w in out_widths],
        compiler_params=_cparams(("parallel",)),
        name=name,
    )(xa, gain.reshape(1, d), mod, *weights, *tables)


def _attn_kernel(q_ref, kl_ref, kc_ref, vl_ref, vc_ref, o_ref, *, n_lat_q):
    q = q_ref[...]

    def finish(ss, vs):
        m = ss[0].max(axis=-1, keepdims=True)
        for s in ss[1:]:
            m = jnp.maximum(m, s.max(axis=-1, keepdims=True))
        ps = [jnp.exp(s - m) for s in ss]
        l = ps[0].sum(axis=-1, keepdims=True)
        for p in ps[1:]:
            l = l + p.sum(axis=-1, keepdims=True)
        o = jnp.dot(ps[0].astype(BF16), vs[0], preferred_element_type=F32)
        for p, v in zip(ps[1:], vs[1:]):
            o = o + jnp.dot(p.astype(BF16), v, preferred_element_type=F32)
        o_ref[...] = (o / l).astype(BF16)

    def latent():
        s_c = lax.dot_general(q, kc_ref[...], _NT, preferred_element_type=F32)
        s_l = lax.dot_general(q, kl_ref[...], _NT, preferred_element_type=F32)
        finish([s_l, s_c], [vl_ref[...], vc_ref[...]])

    def context():
        s_c = lax.dot_general(q, kc_ref[...], _NT, preferred_element_type=F32)
        finish([s_c], [vc_ref[...]])

    qi = pl.program_id(2)
    pl.when(qi < n_lat_q)(latent)
    pl.when(qi >= n_lat_q)(context)


def _attention(q, k, v, n_batch, seq, ctx, n_heads, n_kv_heads, dqk, dv, with_ctx_queries):
    n_rows = q.shape[0]
    tq = ATTN_TQ
    nq = seq // tq
    group = n_heads // n_kv_heads
    ctx_block0 = n_batch * seq // ctx
    grid_q = nq + 1 if with_ctx_queries else nq

    def qmap(b, h, i):
        return (jnp.where(i < nq, b * nq + i, ctx_block0 + b), h)

    def klat(b, h, i):
        return (b, h // group)

    def kctx(b, h, i):
        return (ctx_block0 + b, h // group)

    return pl.pallas_call(
        functools.partial(_attn_kernel, n_lat_q=nq),
        grid=(n_batch, n_heads, grid_q),
        in_specs=[pl.BlockSpec((tq, dqk), qmap),
                  pl.BlockSpec((seq, dqk), klat),
                  pl.BlockSpec((ctx, dqk), kctx),
                  pl.BlockSpec((seq, dv), klat),
                  pl.BlockSpec((ctx, dv), kctx)],
        out_specs=pl.BlockSpec((tq, dv), qmap),
        out_shape=jax.ShapeDtypeStruct(
            (n_rows if with_ctx_queries else n_batch * seq, n_heads * dv), BF16),
        compiler_params=_cparams(("parallel", "parallel", "arbitrary")),
        name="attention",
    )(q, k, k, v, v)


def _post_kernel(attn_ref, x_ref, mod_ref, wo_ref, g_ref, wq_ref, k1_ref, k2_ref,
                 x1_out, h_out, s1_out, s2_out):
    y = jnp.dot(attn_ref[...], wo_ref[...], preferred_element_type=F32)
    x1 = x_ref[...] + mod_ref[0, 2] * y
    x1_out[...] = x1
    h = _rmsn(x1, g_ref[...])
    hb = (h * (1.0 + mod_ref[0, 4]) + mod_ref[0, 3]).astype(BF16)
    h_out[...] = hb
    qp = jnp.dot(hb, wq_ref[...], preferred_element_type=F32).astype(BF16)
    hw = 2 * PEER_HALF
    for hd in range(PEER_HEADS):
        s1_out[hd] = lax.dot_general(k1_ref[hd], qp[:, hd * hw:hd * hw + PEER_HALF], _NT,
                                     preferred_element_type=F32)
        s2_out[hd] = lax.dot_general(k2_ref[hd], qp[:, hd * hw + PEER_HALF:(hd + 1) * hw], _NT,
                                     preferred_element_type=F32)


def _post_attention(attn, xa, mod, w_o, gain, w_query, k1, k2, n_rows, geom):
    d = xa.shape[1]
    tm = ROW_TILE
    row, const, modmap, _ = _row_maps(*geom)

    def const3(i):
        return (0, 0, 0)

    def smap(i):
        return (0, 0, i)

    sshape = jax.ShapeDtypeStruct((PEER_HEADS, PEER_KEYS, n_rows), F32)
    return pl.pallas_call(
        _post_kernel,
        grid=(n_rows // tm,),
        in_specs=[pl.BlockSpec((tm, attn.shape[1]), row),
                  pl.BlockSpec((tm, d), row),
                  pl.BlockSpec((1, N_MOD, 1, d), modmap),
                  pl.BlockSpec(w_o.shape, const),
                  pl.BlockSpec((1, d), const),
                  pl.BlockSpec(w_query.shape, const),
                  pl.BlockSpec(k1.shape, const3),
                  pl.BlockSpec(k2.shape, const3)],
        out_specs=[pl.BlockSpec((tm, d), row),
                   pl.BlockSpec((tm, d), row),
                   pl.BlockSpec((PEER_HEADS, PEER_KEYS, tm), smap),
                   pl.BlockSpec((PEER_HEADS, PEER_KEYS, tm), smap)],
        out_shape=[jax.ShapeDtypeStruct((n_rows, d), F32),
                   jax.ShapeDtypeStruct((n_rows, d), BF16),
                   sshape, sshape],
        compiler_params=_cparams(("parallel",)),
        name="post_attention",
    )(attn, xa, mod, w_o, gain.reshape(1, d), w_query, k1, k2)


def _cmpx(v, i, j):
    a, b = v[i], v[j]
    if b is None:
        return
    if a is None:
        v[i], v[j] = b, None
        return
    v[i], v[j] = jnp.maximum(a, b), jnp.minimum(a, b)


def _bitonic_sort_desc(v):
    n = len(v)
    k = 2
    while k <= n:
        j = k // 2
        while j >= 1:
            for i in range(n):
                l = i ^ j
                if l > i:
                    if (i & k) == 0:
                        _cmpx(v, i, l)
                    else:
                        _cmpx(v, l, i)
            j //= 2
        k *= 2
    return v


def _bitonic_merge_desc(v):
    n = len(v)
    j = n // 2
    while j >= 1:
        for i in range(n):
            l = i ^ j
            if l > i:
                _cmpx(v, i, l)
        j //= 2
    return v


def _max_none(a, b):
    if a is None:
        return b
    if b is None:
        return a
    return jnp.maximum(a, b)


def _merge_top(x, y):
    n = len(x)
    return _bitonic_merge_desc([_max_none(x[i], y[n - 1 - i]) for i in range(n)])


def _top16_rows(s):
    v = [s[r * SUBLANES:(r + 1) * SUBLANES, :] for r in range(PEER_KEYS // SUBLANES)]
    v = _bitonic_sort_desc(v)
    for shift in (4, 2, 1):
        w = [pltpu.roll(x, shift, 0) for x in v]
        v = _merge_top(v, w)
    return v


def _route_kernel(s1_ref, s2_ref, th_out, e1_out, e2_out):
    k = PEER_TOPK

    def head(hd, carry):
        s1 = s1_ref[hd]
        s2 = s2_ref[hd]
        a = _top16_rows(s1)
        b = _top16_rows(s2)
        row0 = [a[0] + b[q] for q in range(k)]
        rest = [a[p] + b[q] for p in range(1, k) for q in range(k // (p + 1))]
        l1 = _bitonic_sort_desc(rest[:k])
        l2 = _bitonic_sort_desc(rest[k:2 * k])
        l3 = _bitonic_sort_desc(rest[2 * k:] + [None] * (3 * k - len(rest)))
        top = _merge_top(_merge_top(row0, l1), _merge_top(l2, l3))
        tau = top[k - 1]
        z = jnp.ones_like(tau)
        for t in top[1:]:
            z = z + jnp.exp(t - top[0])
        rz = 1.0 / z
        n_grp = PEER_KEYS // SUBLANES
        for r in range(n_grp):
            rows = slice(r * SUBLANES, (r + 1) * SUBLANES)
            s1v = s1[rows, :]
            th = jnp.full_like(s1v, jnp.inf)
            for q in range(k):
                th = jnp.where(s1v + b[q] >= tau, b[q], th)
            th_out[hd, rows, :] = th
            e1_out[hd, rows, :] = jnp.exp(s1v - a[0])
            e2_out[hd, rows, :] = jnp.exp(s2[rows, :] - b[0]) * rz
        return carry

    lax.fori_loop(0, PEER_HEADS, head, 0)


def _route(s1t, s2t):
    n_rows = s1t.shape[2]
    tl = ROUTE_TL
    spec = pl.BlockSpec((PEER_HEADS, PEER_KEYS, tl), lambda i: (0, 0, i))
    shape = jax.ShapeDtypeStruct(s1t.shape, F32)
    return pl.pallas_call(
        _route_kernel,
        grid=(n_rows // tl,),
        in_specs=[spec, spec],
        out_specs=[spec, spec, spec],
        out_shape=[shape, shape, shape],
        compiler_params=_cparams(("parallel",)),
        name="peer_route",
    )(s1t, s2t)


def _expert_kernel(h_ref, u_ref, vt_ref, s2_ref, e2_ref, th_ref, e1_ref, x_ref, mod_ref, g_ref,
                   o_ref, acc_ref, *, final_norm):
    ei = pl.program_id(1)

    @pl.when(ei == 0)
    def _():
        acc_ref[...] = jnp.zeros_like(acc_ref)

    act = lax.dot_general(u_ref[...], h_ref[...], _NT, preferred_element_type=F32)
    sqrt_half = math.sqrt(0.5)
    blocks = []
    for il in range(EXP_TE // PEER_KEYS):
        gate = None
        for hd in range(PEER_HEADS):
            th = th_ref[hd, il:il + 1, :]
            e1 = e1_ref[hd, il:il + 1, :]
            term = jnp.where(s2_ref[hd] >= th, e2_ref[hd], 0.0) * e1
            gate = term if gate is None else gate + term
        a = act[il * PEER_KEYS:(il + 1) * PEER_KEYS, :]
        gelu = 0.5 * a * (1.0 + lax.erf(a * sqrt_half))
        blocks.append((gate * gelu).astype(BF16))
    ga = jnp.concatenate(blocks, axis=0)
    acc_ref[...] += jnp.dot(vt_ref[...], ga, preferred_element_type=F32)

    @pl.when(ei == pl.num_programs(1) - 1)
    def _():
        x2 = x_ref[...] + mod_ref[0, 5] * acc_ref[...].T
        if final_norm:
            x2 = _rmsn(x2, g_ref[...])
        o_ref[...] = x2


def _experts(hb, u, vt, s2t, e2t, tht, e1t, x1, mod, gain, n_rows, tiles_per_batch, n_batch,
             final_norm):
    d = x1.shape[1]
    tm, te = EXP_TM, EXP_TE
    n_exp = u.shape[0]
    grp = te // PEER_KEYS

    def tok(t, e):
        return (t, 0)

    def stok(t, e):
        return (0, 0, t)

    def sexp(t, e):
        return (0, e, t)

    return pl.pallas_call(
        functools.partial(_expert_kernel, final_norm=final_norm),
        grid=(n_rows // tm, n_exp // te),
        in_specs=[pl.BlockSpec((tm, d), tok),
                  pl.BlockSpec((te, d), lambda t, e: (e, 0)),
                  pl.BlockSpec((d, te), lambda t, e: (0, e)),
                  pl.BlockSpec((PEER_HEADS, PEER_KEYS, tm), stok),
                  pl.BlockSpec((PEER_HEADS, PEER_KEYS, tm), stok),
                  pl.BlockSpec((PEER_HEADS, grp, tm), sexp),
                  pl.BlockSpec((PEER_HEADS, grp, tm), sexp),
                  pl.BlockSpec((tm, d), tok),
                  pl.BlockSpec((1, N_MOD, 1, d),
                               lambda t, e: (jnp.minimum(t // tiles_per_batch, n_batch), 0, 0, 0)),
                  pl.BlockSpec((1, d), lambda t, e: (0, 0))],
        out_specs=pl.BlockSpec((tm, d), tok),
        out_shape=jax.ShapeDtypeStruct((n_rows, d), F32),
        scratch_shapes=[pltpu.VMEM((d, tm), F32)],
        compiler_params=_cparams(("parallel", "arbitrary")),
        name="peer_experts",
    )(hb, u, vt, s2t, e2t, tht, e1t, x1, mod, gain.reshape(1, d))


def _prep_mla(w_in, w_uq, w_ukv):
    d = w_in.shape[0]
    w_in_p = jnp.concatenate([w_in, jnp.zeros((d, LANES - MLA_ROPE), w_in.dtype)], axis=1)
    w_uq_p = jnp.pad(w_uq.reshape(MLA_Q_RANK, MLA_HEADS, MLA_NOPE + MLA_ROPE),
                     ((0, 0), (0, 0), (0, 2 * LANES - MLA_NOPE - MLA_ROPE)))
    w_uq_p = w_uq_p.reshape(MLA_Q_RANK, MLA_HEADS * 2 * LANES)
    w_ukv_p = w_ukv.reshape(MLA_KV_RANK, MLA_HEADS, 2, MLA_NOPE).transpose(0, 2, 1, 3)
    w_ukv_p = w_ukv_p.reshape(MLA_KV_RANK, 2 * MLA_HEADS * MLA_NOPE)
    return w_in_p.astype(BF16), w_uq_p.astype(BF16), w_ukv_p.astype(BF16)


def kernel(x, c, ctx, c_ctx, l0_ada_w, l0_ada_b, l0_norm_mix, l0_norm_ffn, l0_mla_w_in, l0_mla_q_norm, l0_mla_w_uq, l0_mla_kv_norm, l0_mla_w_ukv, l0_mla_w_o, l0_peer_w_query, l0_peer_k1, l0_peer_k2, l0_peer_u, l0_peer_v, l1_ada_w, l1_ada_b, l1_norm_mix, l1_norm_ffn, l1_gqa_w_qkv, l1_gqa_q_norm, l1_gqa_k_norm, l1_gqa_w_o, l1_peer_w_query, l1_peer_k1, l1_peer_k2, l1_peer_u, l1_peer_v, norm_out):
    n_batch, seq, d = x.shape
    n_ctx = ctx.shape[1]
    assert n_ctx == ATTN_TQ and seq % ATTN_TQ == 0 and seq % GRID_W == 0
    assert seq % ROW_TILE == 0 and (n_batch * n_ctx) % ROW_TILE == 0
    assert seq % EXP_TM == 0 and (n_batch * n_ctx) % EXP_TM == 0 and ROW_TILE == EXP_TM
    n_lat = n_batch * seq
    n_all = n_lat + n_batch * n_ctx
    tiles_per_batch = seq // ROW_TILE
    geom = (n_lat // ROW_TILE, tiles_per_batch, n_batch, seq // ROW_TILE)

    xa = jnp.concatenate([x.reshape(n_lat, d), ctx.reshape(n_batch * n_ctx, d)], axis=0)
    cc = jnp.zeros((SUBLANES, d), F32).at[:n_batch].set(c).at[n_batch].set(c_ctx)
    mod0 = _modulation(cc, l0_ada_w, l0_ada_b).reshape(SUBLANES, N_MOD, 1, d)
    mod1 = _modulation(cc, l1_ada_w, l1_ada_b).reshape(SUBLANES, N_MOD, 1, d)

    w_in_p, w_uq_p, w_ukv_p = _prep_mla(l0_mla_w_in, l0_mla_w_uq, l0_mla_w_ukv)
    tables0 = _rope_tables(seq, MLA_ROPE, ROW_TILE)
    q, k, v = _mixer_proj(
        _mla_proj_kernel, xa, l0_norm_mix, mod0,
        [w_in_p, l0_mla_q_norm.reshape(1, -1), w_uq_p, l0_mla_kv_norm.reshape(1, -1), w_ukv_p],
        tables0, [MLA_HEADS * 2 * LANES, MLA_HEADS * 2 * LANES, MLA_HEADS * MLA_V], geom,
        "mla_proj")
    attn = _attention(q, k, v, n_batch, seq, n_ctx, MLA_HEADS, MLA_HEADS, 2 * LANES, MLA_V, True)
    x1, hb, s1t, s2t = _post_attention(
        attn, xa, mod0, l0_mla_w_o.astype(BF16), l0_norm_ffn, l0_peer_w_query.astype(BF16),
        l0_peer_k1.astype(BF16), l0_peer_k2.astype(BF16), n_all, geom)
    tht, e1t, e2t = _route(s1t, s2t)
    xa = _experts(hb, l0_peer_u.astype(BF16), l0_peer_v.T.astype(BF16), s2t, e2t, tht, e1t, x1,
                  mod0, norm_out, n_all, tiles_per_batch, n_batch, False)

    tables1 = _rope_tables(seq, GQA_DIM, ROW_TILE)
    q, k, v = _mixer_proj(
        _gqa_proj_kernel, xa, l1_norm_mix, mod1,
        [l1_gqa_w_qkv.astype(BF16), l1_gqa_q_norm.reshape(1, -1), l1_gqa_k_norm.reshape(1, -1)],
        tables1, [GQA_HEADS * GQA_DIM, GQA_KV_HEADS * GQA_DIM, GQA_KV_HEADS * GQA_DIM], geom,
        "gqa_proj")
    attn = _attention(q, k, v, n_batch, seq, n_ctx, GQA_HEADS, GQA_KV_HEADS, GQA_DIM, GQA_DIM,
                      False)
    x1, hb, s1t, s2t = _post_attention(
        attn, xa, mod1, l1_gqa_w_o.astype(BF16), l1_norm_ffn, l1_peer_w_query.astype(BF16),
        l1_peer_k1.astype(BF16), l1_peer_k2.astype(BF16), n_lat, geom)
    tht, e1t, e2t = _route(s1t, s2t)
    out = _experts(hb, l1_peer_u.astype(BF16), l1_peer_v.T.astype(BF16), s2t, e2t, tht, e1t, x1,
                   mod1, norm_out, n_lat, tiles_per_batch, n_batch, True)
    return out.reshape(n_batch, seq, d)
```

```python
import functools
import math

import jax
import jax.numpy as jnp
from jax import lax
from jax.experimental import pallas as pl
from jax.experimental.pallas import tpu as pltpu

F32 = jnp.float32
BF16 = jnp.bfloat16

GRID_W = 64
NORM_EPS = 1e-6
ROPE_THETA = 10000.0
N_MOD = 6
MLA_HEADS = 8
MLA_Q_RANK = 384
MLA_KV_RANK = 256
MLA_NOPE = 128
MLA_ROPE = 64
MLA_V = 128
GQA_HEADS = 8
GQA_KV_HEADS = 2
GQA_DIM = 128
PEER_HEADS = 8
PEER_KEYS = 128
PEER_HALF = 128
PEER_TOPK = 16

LANES = 128
SUBLANES = 8
BF16_ROWS = 16
MXU_DIM = 256
VMEM_LIMIT = 56 * 1024 * 1024

ROW_TILE = 512
ATTN_TQ = 1024
ATTN_CHAIN = 256
ROUTE_TL = 128
EXP_TM = 512
EXP_TE = 1024
EXP_SUB = MXU_DIM

_NT = (((1,), (1,)), ((), ()))
LOG2E = math.log2(math.e)


def _cparams(sem):
    return pltpu.CompilerParams(dimension_semantics=sem, vmem_limit_bytes=VMEM_LIMIT)


def _rmsn(x, g):
    ms = jnp.mean(x * x, axis=-1, keepdims=True)
    return x * lax.rsqrt(ms + NORM_EPS) * g


def _mod_kernel(c_ref, w_ref, b_ref, o_ref):
    s = jax.nn.silu(c_ref[...])
    o_ref[...] = jnp.dot(s, w_ref[...], preferred_element_type=F32,
                         precision=lax.Precision.HIGHEST) + b_ref[...]


def _modulation(cc, w, b):
    d, n = w.shape
    bn = n // 4
    return pl.pallas_call(
        _mod_kernel,
        grid=(n // bn,),
        in_specs=[pl.BlockSpec((SUBLANES, d), lambda j: (0, 0)),
                  pl.BlockSpec((d, bn), lambda j: (0, j)),
                  pl.BlockSpec((1, bn), lambda j: (0, j))],
        out_specs=pl.BlockSpec((SUBLANES, bn), lambda j: (0, j)),
        out_shape=jax.ShapeDtypeStruct((SUBLANES, n), F32),
        compiler_params=_cparams(("arbitrary",)),
        name="adaln_mod",
    )(cc, w, b.reshape(1, n))


def _rope_tables(seq, rot_dim, extra_rows):
    rows = seq // GRID_W
    row = jnp.repeat(jnp.arange(rows, dtype=F32), GRID_W)
    col = jnp.tile(jnp.arange(GRID_W, dtype=F32), rows)
    d_axis = rot_dim // 2
    inv_freq = ROPE_THETA ** (-jnp.arange(0, d_axis, 2, dtype=F32) / d_axis)
    ang_r = row[:, None] * inv_freq
    ang_c = col[:, None] * inv_freq
    ang = jnp.concatenate([ang_r, ang_r, ang_c, ang_c], axis=-1)
    cos, sin = jnp.cos(ang), jnp.sin(ang)
    pad = LANES - rot_dim
    cos = jnp.pad(cos, ((0, extra_rows), (0, pad)), constant_values=1.0)
    sin = jnp.pad(sin, ((0, extra_rows), (0, pad)))
    quarter = rot_dim // 4
    first = (jnp.arange(LANES) % (2 * quarter)) < quarter
    sin_a = jnp.where(first[None, :], -sin, 0.0)
    sin_b = jnp.where(first[None, :], 0.0, sin)
    return cos, sin_a, sin_b


def _rope(c, cos, sin_a, sin_b, quarter):
    return (c * cos + pltpu.roll(c, LANES - quarter, 1) * sin_a
            + pltpu.roll(c, quarter, 1) * sin_b)


def _mla_proj_kernel(x_ref, g_ref, mod_ref, win_ref, qn_ref, wuq_ref, kvn_ref, wukv_ref,
                     cos_ref, sa_ref, sb_ref, q_out, k_out, v_out):
    h = _rmsn(x_ref[...], g_ref[...])
    h = h * (1.0 + mod_ref[0, 1]) + mod_ref[0, 0]
    a = jnp.dot(h.astype(BF16), win_ref[...], preferred_element_type=F32)
    cq = _rmsn(a[:, :MLA_Q_RANK], qn_ref[...]).astype(BF16)
    ckv = _rmsn(a[:, MLA_Q_RANK:MLA_Q_RANK + MLA_KV_RANK], kvn_ref[...]).astype(BF16)
    cos, sa, sb = cos_ref[...], sa_ref[...], sb_ref[...]
    quarter = MLA_ROPE // 4
    kr = _rope(a[:, MLA_Q_RANK + MLA_KV_RANK:], cos, sa, sb, quarter).astype(BF16)
    q = jnp.dot(cq, wuq_ref[...], preferred_element_type=F32)
    kv = jnp.dot(ckv, wukv_ref[...], preferred_element_type=F32)
    scale = LOG2E / math.sqrt(MLA_NOPE + MLA_ROPE)
    hw = 2 * LANES
    for hd in range(MLA_HEADS):
        q_out[:, hd * hw:hd * hw + LANES] = (q[:, hd * hw:hd * hw + LANES] * scale).astype(BF16)
        qr = _rope(q[:, hd * hw + LANES:(hd + 1) * hw], cos, sa, sb, quarter) * scale
        q_out[:, hd * hw + LANES:(hd + 1) * hw] = qr.astype(BF16)
        k_out[:, hd * hw:hd * hw + LANES] = kv[:, hd * LANES:(hd + 1) * LANES].astype(BF16)
        k_out[:, hd * hw + LANES:(hd + 1) * hw] = kr
    v_out[...] = kv[:, MLA_HEADS * LANES:].astype(BF16)


def _gqa_proj_kernel(x_ref, g_ref, mod_ref, wqkv_ref, qn_ref, kn_ref,
                     cos_ref, sa_ref, sb_ref, q_out, k_out, v_out):
    h = _rmsn(x_ref[...], g_ref[...])
    h = h * (1.0 + mod_ref[0, 1]) + mod_ref[0, 0]
    qkv = jnp.dot(h.astype(BF16), wqkv_ref[...], preferred_element_type=F32)
    cos, sa, sb = cos_ref[...], sa_ref[...], sb_ref[...]
    quarter = GQA_DIM // 4
    scale = LOG2E / math.sqrt(GQA_DIM)
    for hd in range(GQA_HEADS):
        c = _rmsn(qkv[:, hd * LANES:(hd + 1) * LANES], qn_ref[...])
        q_out[:, hd * LANES:(hd + 1) * LANES] = (_rope(c, cos, sa, sb, quarter) * scale).astype(BF16)
    k0 = GQA_HEADS * LANES
    for hd in range(GQA_KV_HEADS):
        c = _rmsn(qkv[:, k0 + hd * LANES:k0 + (hd + 1) * LANES], kn_ref[...])
        k_out[:, hd * LANES:(hd + 1) * LANES] = _rope(c, cos, sa, sb, quarter).astype(BF16)
    v_out[...] = qkv[:, k0 + GQA_KV_HEADS * LANES:].astype(BF16)


def _row_maps(n_lat_tiles, tiles_per_batch, n_batch, tiles_per_seq):
    def row(i):
        return (i, 0)

    def const(i):
        return (0, 0)

    def mod(i):
        return (jnp.minimum(i // tiles_per_batch, n_batch), 0, 0, 0)

    def rope(i):
        return (jnp.where(i < n_lat_tiles, i % tiles_per_seq, tiles_per_seq), 0)

    return row, const, mod, rope


def _mixer_proj(kern, xa, gain, mod, weights, tables, out_widths, geom, name):
    n_rows, d = xa.shape
    tm = ROW_TILE
    row, const, modmap, ropemap = _row_maps(*geom)
    in_specs = [pl.BlockSpec((tm, d), row),
                pl.BlockSpec((1, d), const),
                pl.BlockSpec((1, N_MOD, 1, d), modmap)]
    in_specs += [pl.BlockSpec(w.shape, const) for w in weights]
    in_specs += [pl.BlockSpec((tm, LANES), ropemap) for _ in tables]
    return pl.pallas_call(
        kern,
        grid=(n_rows // tm,),
        in_specs=in_specs,
        out_specs=[pl.BlockSpec((tm, w), row) for w in out_widths],
        out_shape=[jax.ShapeDtypeStruct((n_rows, w), BF16) for w in out_widths],
        compiler_params=_cparams(("parallel",)),
        name=name,
    )(xa, gain.reshape(1, d), mod, *weights, *tables)


def _softmax_pv(q, ks, vs):
    ss = [lax.dot_general(q, k, _NT, preferred_element_type=F32) for k in ks]
    m = ss[0].max(axis=-1, keepdims=True)
    for s in ss[1:]:
        m = jnp.maximum(m, s.max(axis=-1, keepdims=True))
    ps = [jnp.exp2(s - m) for s in ss]
    l = ps[0].sum(axis=-1, keepdims=True)
    for p in ps[1:]:
        l = l + p.sum(axis=-1, keepdims=True)
    o = jnp.dot(ps[0].astype(BF16), vs[0], preferred_element_type=F32)
    for p, v in zip(ps[1:], vs[1:]):
        o = o + jnp.dot(p.astype(BF16), v, preferred_element_type=F32)
    return (o / l).astype(BF16)


def _attn_kernel(q_ref, kl_ref, kc_ref, vl_ref, vc_ref, o_ref, s_ref, p_ref):
    n_lat = kl_ref.shape[0]
    for c in range(ATTN_TQ // ATTN_CHAIN):
        rows = slice(c * ATTN_CHAIN, (c + 1) * ATTN_CHAIN)
        q = q_ref[rows, :]
        s_ref[rows, :n_lat] = lax.dot_general(q, kl_ref[...], _NT, preferred_element_type=F32)
        s_ref[rows, n_lat:] = lax.dot_general(q, kc_ref[...], _NT, preferred_element_type=F32)
        s = s_ref[rows, :]
        p = jnp.exp2(s - s.max(axis=-1, keepdims=True))
        l = p.sum(axis=-1, keepdims=True)
        p_ref[rows, :] = p.astype(BF16)
        o = (jnp.dot(p_ref[rows, :n_lat], vl_ref[...], preferred_element_type=F32)
             + jnp.dot(p_ref[rows, n_lat:], vc_ref[...], preferred_element_type=F32))
        o_ref[rows, :] = (o / l).astype(BF16)


def _attn_ctx_kernel(q_ref, kc_ref, vc_ref, prev_ref, o_ref):
    del prev_ref
    o_ref[...] = _softmax_pv(q_ref[...], [kc_ref[...]], [vc_ref[...]])


def _attention(q, k, v, n_batch, seq, ctx, n_heads, n_kv_heads, dqk, dv, with_ctx_queries):
    n_rows = q.shape[0]
    tq = ATTN_TQ
    nq = seq // tq
    group = n_heads // n_kv_heads
    ctx_block0 = n_batch * seq // ctx
    out_rows = n_rows if with_ctx_queries else n_batch * seq

    def qmap(b, h, i):
        return (b * nq + i, h)

    def klat(b, h, i):
        return (b, h // group)

    def kctx(b, h, i):
        return (ctx_block0 + b, h // group)

    out = pl.pallas_call(
        _attn_kernel,
        grid=(n_batch, n_heads, nq),
        in_specs=[pl.BlockSpec((tq, dqk), qmap),
                  pl.BlockSpec((seq, dqk), klat),
                  pl.BlockSpec((ctx, dqk), kctx),
                  pl.BlockSpec((seq, dv), klat),
                  pl.BlockSpec((ctx, dv), kctx)],
        out_specs=pl.BlockSpec((tq, dv), qmap),
        out_shape=jax.ShapeDtypeStruct((out_rows, n_heads * dv), BF16),
        scratch_shapes=[pltpu.VMEM((tq, seq + ctx), F32), pltpu.VMEM((tq, seq + ctx), BF16)],
        compiler_params=_cparams(("parallel", "parallel", "arbitrary")),
        name="attention",
    )(q, k, k, v, v)
    if not with_ctx_queries:
        return out

    def ckv(b, h):
        return (ctx_block0 + b, h // group)

    def cq(b, h):
        return (ctx_block0 + b, h)

    return pl.pallas_call(
        _attn_ctx_kernel,
        grid=(n_batch, n_heads),
        in_specs=[pl.BlockSpec((ctx, dqk), cq),
                  pl.BlockSpec((ctx, dqk), ckv),
                  pl.BlockSpec((ctx, dv), ckv),
                  pl.BlockSpec(memory_space=pl.ANY)],
        out_specs=pl.BlockSpec((ctx, dv), cq),
        out_shape=jax.ShapeDtypeStruct(out.shape, BF16),
        input_output_aliases={3: 0},
        compiler_params=_cparams(("parallel", "parallel")),
        name="attention_ctx",
    )(q, k, v, out)


def _post_kernel(attn_ref, x_ref, mod_ref, wo_ref, g_ref, wq_ref, k1_ref, k2_ref,
                 x1_out, h_out, s1_out, s2_out):
    y = jnp.dot(attn_ref[...], wo_ref[...], preferred_element_type=F32)
    x1 = x_ref[...] + mod_ref[0, 2] * y
    x1_out[...] = x1
    h = _rmsn(x1, g_ref[...])
    hb = (h * (1.0 + mod_ref[0, 4]) + mod_ref[0, 3]).astype(BF16)
    h_out[...] = hb
    qp = jnp.dot(hb, wq_ref[...], preferred_element_type=F32).astype(BF16)
    hw = 2 * PEER_HALF
    for hd in range(PEER_HEADS):
        s1_out[hd] = lax.dot_general(k1_ref[hd], qp[:, hd * hw:hd * hw + PEER_HALF], _NT,
                                     preferred_element_type=F32)
        s2_out[hd] = lax.dot_general(k2_ref[hd], qp[:, hd * hw + PEER_HALF:(hd + 1) * hw], _NT,
                                     preferred_element_type=F32)


def _post_attention(attn, xa, mod, w_o, gain, w_query, k1, k2, n_rows, geom):
    d = xa.shape[1]
    tm = ROW_TILE
    row, const, modmap, _ = _row_maps(*geom)

    def const3(i):
        return (0, 0, 0)

    def smap(i):
        return (0, 0, i)

    sshape = jax.ShapeDtypeStruct((PEER_HEADS, PEER_KEYS, n_rows), F32)
    return pl.pallas_call(
        _post_kernel,
        grid=(n_rows // tm,),
        in_specs=[pl.BlockSpec((tm, attn.shape[1]), row),
                  pl.BlockSpec((tm, d), row),
                  pl.BlockSpec((1, N_MOD, 1, d), modmap),
                  pl.BlockSpec(w_o.shape, const),
                  pl.BlockSpec((1, d), const),
                  pl.BlockSpec(w_query.shape, const),
                  pl.BlockSpec(k1.shape, const3),
                  pl.BlockSpec(k2.shape, const3)],
        out_specs=[pl.BlockSpec((tm, d), row),
                   pl.BlockSpec((tm, d), row),
                   pl.BlockSpec((PEER_HEADS, PEER_KEYS, tm), smap),
                   pl.BlockSpec((PEER_HEADS, PEER_KEYS, tm), smap)],
        out_shape=[jax.ShapeDtypeStruct((n_rows, d), F32),
                   jax.ShapeDtypeStruct((n_rows, d), BF16),
                   sshape, sshape],
        compiler_params=_cparams(("parallel",)),
        name="post_attention",
    )(attn, xa, mod, w_o, gain.reshape(1, d), w_query, k1, k2)


def _cmpx(v, i, j):
    a, b = v[i], v[j]
    if b is None:
        return
    if a is None:
        v[i], v[j] = b, None
        return
    v[i], v[j] = jnp.maximum(a, b), jnp.minimum(a, b)


def _bitonic_sort_desc(v):
    n = len(v)
    k = 2
    while k <= n:
        j = k // 2
        while j >= 1:
            for i in range(n):
                l = i ^ j
                if l > i:
                    if (i & k) == 0:
                        _cmpx(v, i, l)
                    else:
                        _cmpx(v, l, i)
            j //= 2
        k *= 2
    return v


def _bitonic_merge_desc(v):
    n = len(v)
    j = n // 2
    while j >= 1:
        for i in range(n):
            l = i ^ j
            if l > i:
                _cmpx(v, i, l)
        j //= 2
    return v


def _max_none(a, b):
    if a is None:
        return b
    if b is None:
        return a
    return jnp.maximum(a, b)


def _merge_top(x, y):
    n = len(x)
    return _bitonic_merge_desc([_max_none(x[i], y[n - 1 - i]) for i in range(n)])


def _top16_rows(s):
    v = [s[r * SUBLANES:(r + 1) * SUBLANES, :] for r in range(PEER_KEYS // SUBLANES)]
    v = _bitonic_sort_desc(v)
    for shift in (4, 2, 1):
        w = [pltpu.roll(x, shift, 0) for x in v]
        v = _merge_top(v, w)
    return v


def _route_kernel(s1_ref, s2_ref, cnt_out, e1_out, rank_out, e2_out):
    k = PEER_TOPK

    def head(hd, carry):
        s1 = s1_ref[hd]
        s2 = s2_ref[hd]
        a = _top16_rows(s1)
        b = _top16_rows(s2)
        row0 = [a[0] + b[q] for q in range(k)]
        rest = [a[p] + b[q] for p in range(1, k) for q in range(k // (p + 1))]
        l1 = _bitonic_sort_desc(rest[:k])
        l2 = _bitonic_sort_desc(rest[k:2 * k])
        l3 = _bitonic_sort_desc(rest[2 * k:] + [None] * (3 * k - len(rest)))
        top = _merge_top(_merge_top(row0, l1), _merge_top(l2, l3))
        tau = top[k - 1]
        z = jnp.ones_like(tau)
        for t in top[1:]:
            z = z + jnp.exp(t - top[0])
        half_rz = 0.5 / z
        cnts, e1s, ranks, e2s = [], [], [], []
        for r in range(PEER_KEYS // SUBLANES):
            rows = slice(r * SUBLANES, (r + 1) * SUBLANES)
            s1v, s2v = s1[rows, :], s2[rows, :]
            cnt = jnp.zeros_like(s1v)
            rank = jnp.zeros_like(s2v)
            for q in range(k):
                cnt = cnt + jnp.where(s1v + b[q] >= tau, 1.0, 0.0)
                rank = rank + jnp.where(b[q] > s2v, 1.0, 0.0)
            cnts.append(cnt)
            ranks.append(rank)
            e1s.append(jnp.exp(s1v - a[0]))
            e2s.append(jnp.exp(s2v - b[0]) * half_rz)
        cnt_out[hd] = jnp.concatenate(cnts, axis=0)
        e1_out[hd] = jnp.concatenate(e1s, axis=0)
        rank_out[hd] = jnp.concatenate(ranks, axis=0).astype(BF16)
        e2_out[hd] = jnp.concatenate(e2s, axis=0).astype(BF16)
        return carry

    lax.fori_loop(0, PEER_HEADS, head, 0)


def _route(s1t, s2t):
    n_rows = s1t.shape[2]
    tl = ROUTE_TL
    spec = pl.BlockSpec((PEER_HEADS, PEER_KEYS, tl), lambda i: (0, 0, i))
    f32s = jax.ShapeDtypeStruct(s1t.shape, F32)
    bf16s = jax.ShapeDtypeStruct(s1t.shape, BF16)
    return pl.pallas_call(
        _route_kernel,
        grid=(n_rows // tl,),
        in_specs=[spec, spec],
        out_specs=[spec, spec, spec, spec],
        out_shape=[f32s, f32s, bf16s, bf16s],
        compiler_params=_cparams(("parallel",)),
        name="peer_route",
    )(s1t, s2t)


def _expert_kernel(h_ref, u_ref, vt_ref, rank_ref, e2_ref, cnt_ref, e1_ref, x_ref, mod_ref, g_ref,
                   o_ref, acc_ref, gel_ref, ga_ref, *, final_norm):
    ei = pl.program_id(1)
    tm = h_ref.shape[0]

    @pl.when(ei == 0)
    def _():
        acc_ref[...] = jnp.zeros_like(acc_ref)

    h = h_ref[...]
    sqrt_half = math.sqrt(0.5)
    zero = jnp.zeros((BF16_ROWS, tm), BF16)
    n_rowgrp = PEER_KEYS // BF16_ROWS
    n_sub = EXP_TE // EXP_SUB

    def activations(sb):
        esl = slice(sb * EXP_SUB, (sb + 1) * EXP_SUB)
        act = lax.dot_general(u_ref[esl, :], h, _NT, preferred_element_type=F32)
        gel_ref[esl, :] = (act * (1.0 + lax.erf(act * sqrt_half))).astype(BF16)

    def gated(sb):
        for il in range(sb * EXP_SUB // PEER_KEYS, (sb + 1) * EXP_SUB // PEER_KEYS):
            acc = [None] * n_rowgrp
            for hd in range(PEER_HEADS):
                cnt = jnp.broadcast_to(cnt_ref[hd, il:il + 1, :], (BF16_ROWS, tm)).astype(BF16)
                e1 = jnp.broadcast_to(e1_ref[hd, il:il + 1, :], (BF16_ROWS, tm)).astype(BF16)
                for r in range(n_rowgrp):
                    rows = slice(r * BF16_ROWS, (r + 1) * BF16_ROWS)
                    term = jnp.where(rank_ref[hd, rows, :] < cnt, e2_ref[hd, rows, :], zero) * e1
                    acc[r] = term if acc[r] is None else acc[r] + term
            for r in range(n_rowgrp):
                rows = slice(il * PEER_KEYS + r * BF16_ROWS, il * PEER_KEYS + (r + 1) * BF16_ROWS)
                ga_ref[rows, :] = acc[r] * gel_ref[rows, :]

    activations(0)
    for sb in range(n_sub):
        esl = slice(sb * EXP_SUB, (sb + 1) * EXP_SUB)
        if sb + 1 < n_sub:
            activations(sb + 1)
        gated(sb)
        acc_ref[...] += jnp.dot(vt_ref[:, esl], ga_ref[esl, :], preferred_element_type=F32)

    @pl.when(ei == pl.num_programs(1) - 1)
    def _():
        x2 = x_ref[...] + mod_ref[0, 5] * acc_ref[...].T
        if final_norm:
            x2 = _rmsn(x2, g_ref[...])
        o_ref[...] = x2


def _experts(hb, u, vt, rank, e2, cnt, e1, x1, mod, gain, n_rows, tiles_per_batch, n_batch,
             final_norm):
    d = x1.shape[1]
    tm, te = EXP_TM, EXP_TE
    n_exp = u.shape[0]
    grp = te // PEER_KEYS

    def tok(t, e):
        return (t, 0)

    def stok(t, e):
        return (0, 0, t)

    def sexp(t, e):
        return (0, e, t)

    return pl.pallas_call(
        functools.partial(_expert_kernel, final_norm=final_norm),
        grid=(n_rows // tm, n_exp // te),
        in_specs=[pl.BlockSpec((tm, d), tok),
                  pl.BlockSpec((te, d), lambda t, e: (e, 0)),
                  pl.BlockSpec((d, te), lambda t, e: (0, e)),
                  pl.BlockSpec((PEER_HEADS, PEER_KEYS, tm), stok),
                  pl.BlockSpec((PEER_HEADS, PEER_KEYS, tm), stok),
                  pl.BlockSpec((PEER_HEADS, grp, tm), sexp),
                  pl.BlockSpec((PEER_HEADS, grp, tm), sexp),
                  pl.BlockSpec((tm, d), tok),
                  pl.BlockSpec((1, N_MOD, 1, d),
                               lambda t, e: (jnp.minimum(t // tiles_per_batch, n_batch), 0, 0, 0)),
                  pl.BlockSpec((1, d), lambda t, e: (0, 0))],
        out_specs=pl.BlockSpec((tm, d), tok),
        out_shape=jax.ShapeDtypeStruct((n_rows, d), F32),
        scratch_shapes=[pltpu.VMEM((d, tm), F32), pltpu.VMEM((te, tm), BF16),
                        pltpu.VMEM((te, tm), BF16)],
        compiler_params=_cparams(("parallel", "arbitrary")),
        name="peer_experts",
    )(hb, u, vt, rank, e2, cnt, e1, x1, mod, gain.reshape(1, d))


def _peer(attn, xa, mod, w_o, norm_ffn, w_query, k1, k2, u, v, gain_out, n_rows, geom, final_norm):
    x1, hb, s1t, s2t = _post_attention(attn, xa, mod, w_o.astype(BF16), norm_ffn,
                                       w_query.astype(BF16), k1.astype(BF16), k2.astype(BF16),
                                       n_rows, geom)
    cnt, e1, rank, e2 = _route(s1t, s2t)
    return _experts(hb, u.astype(BF16), v.T.astype(BF16), rank, e2, cnt, e1, x1, mod, gain_out,
                    n_rows, geom[1], geom[2], final_norm)


def _prep_mla(w_in, w_uq, w_ukv):
    d = w_in.shape[0]
    w_in_p = jnp.concatenate([w_in, jnp.zeros((d, LANES - MLA_ROPE), w_in.dtype)], axis=1)
    w_uq_p = jnp.pad(w_uq.reshape(MLA_Q_RANK, MLA_HEADS, MLA_NOPE + MLA_ROPE),
                     ((0, 0), (0, 0), (0, 2 * LANES - MLA_NOPE - MLA_ROPE)))
    w_uq_p = w_uq_p.reshape(MLA_Q_RANK, MLA_HEADS * 2 * LANES)
    w_ukv_p = w_ukv.reshape(MLA_KV_RANK, MLA_HEADS, 2, MLA_NOPE).transpose(0, 2, 1, 3)
    w_ukv_p = w_ukv_p.reshape(MLA_KV_RANK, 2 * MLA_HEADS * MLA_NOPE)
    return w_in_p.astype(BF16), w_uq_p.astype(BF16), w_ukv_p.astype(BF16)


def kernel(x, c, ctx, c_ctx, l0_ada_w, l0_ada_b, l0_norm_mix, l0_norm_ffn, l0_mla_w_in, l0_mla_q_norm, l0_mla_w_uq, l0_mla_kv_norm, l0_mla_w_ukv, l0_mla_w_o, l0_peer_w_query, l0_peer_k1, l0_peer_k2, l0_peer_u, l0_peer_v, l1_ada_w, l1_ada_b, l1_norm_mix, l1_norm_ffn, l1_gqa_w_qkv, l1_gqa_q_norm, l1_gqa_k_norm, l1_gqa_w_o, l1_peer_w_query, l1_peer_k1, l1_peer_k2, l1_peer_u, l1_peer_v, norm_out):
    n_batch, seq, d = x.shape
    n_ctx = ctx.shape[1]
    assert seq % ATTN_TQ == 0 and seq % GRID_W == 0 and (n_batch * seq) % n_ctx == 0
    assert seq % ROW_TILE == 0 and (n_batch * n_ctx) % ROW_TILE == 0
    assert seq % EXP_TM == 0 and (n_batch * n_ctx) % EXP_TM == 0 and ROW_TILE == EXP_TM
    n_lat = n_batch * seq
    n_all = n_lat + n_batch * n_ctx
    geom = (n_lat // ROW_TILE, seq // ROW_TILE, n_batch, seq // ROW_TILE)

    xa = jnp.concatenate([x.reshape(n_lat, d), ctx.reshape(n_batch * n_ctx, d)], axis=0)
    cc = jnp.zeros((SUBLANES, d), F32).at[:n_batch].set(c).at[n_batch].set(c_ctx)
    mod0 = _modulation(cc, l0_ada_w, l0_ada_b).reshape(SUBLANES, N_MOD, 1, d)
    mod1 = _modulation(cc, l1_ada_w, l1_ada_b).reshape(SUBLANES, N_MOD, 1, d)

    w_in_p, w_uq_p, w_ukv_p = _prep_mla(l0_mla_w_in, l0_mla_w_uq, l0_mla_w_ukv)
    q, k, v = _mixer_proj(
        _mla_proj_kernel, xa, l0_norm_mix, mod0,
        [w_in_p, l0_mla_q_norm.reshape(1, -1), w_uq_p, l0_mla_kv_norm.reshape(1, -1), w_ukv_p],
        _rope_tables(seq, MLA_ROPE, ROW_TILE),
        [MLA_HEADS * 2 * LANES, MLA_HEADS * 2 * LANES, MLA_HEADS * MLA_V], geom, "mla_proj")
    attn = _attention(q, k, v, n_batch, seq, n_ctx, MLA_HEADS, MLA_HEADS, 2 * LANES, MLA_V, True)
    xa = _peer(attn, xa, mod0, l0_mla_w_o, l0_norm_ffn, l0_peer_w_query, l0_peer_k1, l0_peer_k2,
               l0_peer_u, l0_peer_v, norm_out, n_all, geom, False)

    q, k, v = _mixer_proj(
        _gqa_proj_kernel, xa, l1_norm_mix, mod1,
        [l1_gqa_w_qkv.astype(BF16), l1_gqa_q_norm.reshape(1, -1), l1_gqa_k_norm.reshape(1, -1)],
        _rope_tables(seq, GQA_DIM, ROW_TILE),
        [GQA_HEADS * GQA_DIM, GQA_KV_HEADS * GQA_DIM, GQA_KV_HEADS * GQA_DIM], geom, "gqa_proj")
    attn = _attention(q, k, v, n_batch, seq, n_ctx, GQA_HEADS, GQA_KV_HEADS, GQA_DIM, GQA_DIM,
                      False)
    out = _peer(attn, xa, mod1, l1_gqa_w_o, l1_norm_ffn, l1_peer_w_query, l1_peer_k1, l1_peer_k2,
                l1_peer_u, l1_peer_v, norm_out, n_lat, geom, True)
    return out.reshape(n_batch, seq, d)
```

```python
import functools
import math

import jax
import jax.numpy as jnp
from jax import lax
from jax.experimental import pallas as pl
from jax.experimental.pallas import tpu as pltpu

F32 = jnp.float32
BF16 = jnp.bfloat16

GRID_W = 64
NORM_EPS = 1e-6
ROPE_THETA = 10000.0
N_MOD = 6
MLA_HEADS = 8
MLA_Q_RANK = 384
MLA_KV_RANK = 256
MLA_NOPE = 128
MLA_ROPE = 64
MLA_V = 128
GQA_HEADS = 8
GQA_KV_HEADS = 2
GQA_DIM = 128
PEER_HEADS = 8
PEER_KEYS = 128
PEER_HALF = 128
PEER_TOPK = 16

LANES = 128
SUBLANES = 8
BF16_ROWS = 16
MXU_DIM = 256
VMEM_LIMIT = 56 * 1024 * 1024

ROW_TILE = 512
ATTN_TQ = 1024
ATTN_CHAIN = 256
ROUTE_TL = 128
EXP_TM = 512
EXP_TE = 1024
EXP_SUB = MXU_DIM

_NT = (((1,), (1,)), ((), ()))
LOG2E = math.log2(math.e)


def _cparams(sem):
    return pltpu.CompilerParams(dimension_semantics=sem, vmem_limit_bytes=VMEM_LIMIT)


def _rmsn(x, g):
    ms = jnp.mean(x * x, axis=-1, keepdims=True)
    return x * lax.rsqrt(ms + NORM_EPS) * g


def _mod_kernel(c_ref, w_ref, b_ref, o_ref):
    s = jax.nn.silu(c_ref[...])
    o_ref[...] = jnp.dot(s, w_ref[...], preferred_element_type=F32,
                         precision=lax.Precision.HIGHEST) + b_ref[...]


def _modulation(cc, w, b):
    d, n = w.shape
    bn = n // 4
    return pl.pallas_call(
        _mod_kernel,
        grid=(n // bn,),
        in_specs=[pl.BlockSpec((SUBLANES, d), lambda j: (0, 0)),
                  pl.BlockSpec((d, bn), lambda j: (0, j)),
                  pl.BlockSpec((1, bn), lambda j: (0, j))],
        out_specs=pl.BlockSpec((SUBLANES, bn), lambda j: (0, j)),
        out_shape=jax.ShapeDtypeStruct((SUBLANES, n), F32),
        compiler_params=_cparams(("arbitrary",)),
        name="adaln_mod",
    )(cc, w, b.reshape(1, n))


def _rope_tables(seq, rot_dim, extra_rows):
    rows = seq // GRID_W
    row = jnp.repeat(jnp.arange(rows, dtype=F32), GRID_W)
    col = jnp.tile(jnp.arange(GRID_W, dtype=F32), rows)
    d_axis = rot_dim // 2
    inv_freq = ROPE_THETA ** (-jnp.arange(0, d_axis, 2, dtype=F32) / d_axis)
    ang_r = row[:, None] * inv_freq
    ang_c = col[:, None] * inv_freq
    ang = jnp.concatenate([ang_r, ang_r, ang_c, ang_c], axis=-1)
    cos, sin = jnp.cos(ang), jnp.sin(ang)
    pad = LANES - rot_dim
    cos = jnp.pad(cos, ((0, extra_rows), (0, pad)), constant_values=1.0)
    sin = jnp.pad(sin, ((0, extra_rows), (0, pad)))
    quarter = rot_dim // 4
    first = (jnp.arange(LANES) % (2 * quarter)) < quarter
    sin_a = jnp.where(first[None, :], -sin, 0.0)
    sin_b = jnp.where(first[None, :], 0.0, sin)
    return cos, sin_a, sin_b


def _rope(c, cos, sin_a, sin_b, quarter):
    return (c * cos + pltpu.roll(c, LANES - quarter, 1) * sin_a
            + pltpu.roll(c, quarter, 1) * sin_b)


def _mla_proj_kernel(x_ref, g_ref, mod_ref, win_ref, qn_ref, wuq_ref, kvn_ref, wukv_ref,
                     cos_ref, sa_ref, sb_ref, q_out, k_out, v_out):
    h = _rmsn(x_ref[...], g_ref[...])
    h = h * (1.0 + mod_ref[0, 1]) + mod_ref[0, 0]
    a = jnp.dot(h.astype(BF16), win_ref[...], preferred_element_type=F32)
    cq = _rmsn(a[:, :MLA_Q_RANK], qn_ref[...]).astype(BF16)
    ckv = _rmsn(a[:, MLA_Q_RANK:MLA_Q_RANK + MLA_KV_RANK], kvn_ref[...]).astype(BF16)
    cos, sa, sb = cos_ref[...], sa_ref[...], sb_ref[...]
    quarter = MLA_ROPE // 4
    kr = _rope(a[:, MLA_Q_RANK + MLA_KV_RANK:], cos, sa, sb, quarter).astype(BF16)
    q = jnp.dot(cq, wuq_ref[...], preferred_element_type=F32)
    kv = jnp.dot(ckv, wukv_ref[...], preferred_element_type=F32)
    scale = LOG2E / math.sqrt(MLA_NOPE + MLA_ROPE)
    hw = 2 * LANES
    ones = jnp.ones((q.shape[0], LANES), BF16)
    for hd in range(MLA_HEADS):
        q_out[:, hd * hw:hd * hw + LANES] = (q[:, hd * hw:hd * hw + LANES] * scale).astype(BF16)
        qr = _rope(q[:, hd * hw + LANES:(hd + 1) * hw], cos, sa, sb, quarter) * scale
        q_out[:, hd * hw + LANES:(hd + 1) * hw] = qr.astype(BF16)
        k_out[:, hd * hw:hd * hw + LANES] = kv[:, hd * LANES:(hd + 1) * LANES].astype(BF16)
        k_out[:, hd * hw + LANES:(hd + 1) * hw] = kr
        v0 = (MLA_HEADS + hd) * LANES
        v_out[:, hd * hw:hd * hw + LANES] = kv[:, v0:v0 + LANES].astype(BF16)
        v_out[:, hd * hw + LANES:(hd + 1) * hw] = ones


def _gqa_proj_kernel(x_ref, g_ref, mod_ref, wqkv_ref, qn_ref, kn_ref,
                     cos_ref, sa_ref, sb_ref, q_out, k_out, v_out):
    h = _rmsn(x_ref[...], g_ref[...])
    h = h * (1.0 + mod_ref[0, 1]) + mod_ref[0, 0]
    qkv = jnp.dot(h.astype(BF16), wqkv_ref[...], preferred_element_type=F32)
    cos, sa, sb = cos_ref[...], sa_ref[...], sb_ref[...]
    quarter = GQA_DIM // 4
    scale = LOG2E / math.sqrt(GQA_DIM)
    for hd in range(GQA_HEADS):
        c = _rmsn(qkv[:, hd * LANES:(hd + 1) * LANES], qn_ref[...])
        q_out[:, hd * LANES:(hd + 1) * LANES] = (_rope(c, cos, sa, sb, quarter) * scale).astype(BF16)
    k0 = GQA_HEADS * LANES
    for hd in range(GQA_KV_HEADS):
        c = _rmsn(qkv[:, k0 + hd * LANES:k0 + (hd + 1) * LANES], kn_ref[...])
        k_out[:, hd * LANES:(hd + 1) * LANES] = _rope(c, cos, sa, sb, quarter).astype(BF16)
    v0 = k0 + GQA_KV_HEADS * LANES
    ones = jnp.ones((qkv.shape[0], LANES), BF16)
    for hd in range(GQA_KV_HEADS):
        v_out[:, 2 * hd * LANES:(2 * hd + 1) * LANES] = (
            qkv[:, v0 + hd * LANES:v0 + (hd + 1) * LANES].astype(BF16))
        v_out[:, (2 * hd + 1) * LANES:(2 * hd + 2) * LANES] = ones


def _row_maps(n_lat_tiles, tiles_per_batch, n_batch, tiles_per_seq):
    def row(i):
        return (i, 0)

    def const(i):
        return (0, 0)

    def mod(i):
        return (jnp.minimum(i // tiles_per_batch, n_batch), 0, 0, 0)

    def rope(i):
        return (jnp.where(i < n_lat_tiles, i % tiles_per_seq, tiles_per_seq), 0)

    return row, const, mod, rope


def _mixer_proj(kern, xa, gain, mod, weights, tables, out_widths, geom, name):
    n_rows, d = xa.shape
    tm = ROW_TILE
    row, const, modmap, ropemap = _row_maps(*geom)
    in_specs = [pl.BlockSpec((tm, d), row),
                pl.BlockSpec((1, d), const),
                pl.BlockSpec((1, N_MOD, 1, d), modmap)]
    in_specs += [pl.BlockSpec(w.shape, const) for w in weights]
    in_specs += [pl.BlockSpec((tm, LANES), ropemap) for _ in tables]
    return pl.pallas_call(
        kern,
        grid=(n_rows // tm,),
        in_specs=in_specs,
        out_specs=[pl.BlockSpec((tm, w), row) for w in out_widths],
        out_shape=[jax.ShapeDtypeStruct((n_rows, w), BF16) for w in out_widths],
        compiler_params=_cparams(("parallel",)),
        name=name,
    )(xa, gain.reshape(1, d), mod, *weights, *tables)


def _attn_kernel(q_ref, kl_ref, kc_ref, vl_ref, vc_ref, o_ref, s_ref, p_ref):
    n_lat = kl_ref.shape[0]
    dv = o_ref.shape[1]
    for c in range(ATTN_TQ // ATTN_CHAIN):
        rows = slice(c * ATTN_CHAIN, (c + 1) * ATTN_CHAIN)
        q = q_ref[rows, :]
        s_ref[rows, :n_lat] = lax.dot_general(q, kl_ref[...], _NT, preferred_element_type=F32)
        s_ref[rows, n_lat:] = lax.dot_general(q, kc_ref[...], _NT, preferred_element_type=F32)
        s = s_ref[rows, :]
        p_ref[rows, :] = jnp.exp2(s - s.max(axis=-1, keepdims=True)).astype(BF16)
        o = (jnp.dot(p_ref[rows, :n_lat], vl_ref[...], preferred_element_type=F32)
             + jnp.dot(p_ref[rows, n_lat:], vc_ref[...], preferred_element_type=F32))
        o_ref[rows, :] = (o[:, :dv] / o[:, dv:]).astype(BF16)


def _attn_ctx_kernel(q_ref, kc_ref, vc_ref, o_ref):
    dv = o_ref.shape[1]
    s = lax.dot_general(q_ref[...], kc_ref[...], _NT, preferred_element_type=F32)
    p = jnp.exp2(s - s.max(axis=-1, keepdims=True)).astype(BF16)
    o = jnp.dot(p, vc_ref[...], preferred_element_type=F32)
    o_ref[...] = (o[:, :dv] / o[:, dv:]).astype(BF16)


def _attention(q, k, v, n_batch, seq, ctx, n_heads, n_kv_heads, dqk, dv, with_ctx_queries):
    tq = ATTN_TQ
    nq = seq // tq
    group = n_heads // n_kv_heads
    ctx_block0 = n_batch * seq // ctx
    dve = 2 * dv

    def qmap(b, h, i):
        return (b * nq + i, h)

    def klat(b, h, i):
        return (b, h // group)

    def kctx(b, h, i):
        return (ctx_block0 + b, h // group)

    out = pl.pallas_call(
        _attn_kernel,
        grid=(n_batch, n_heads, nq),
        in_specs=[pl.BlockSpec((tq, dqk), qmap),
                  pl.BlockSpec((seq, dqk), klat),
                  pl.BlockSpec((ctx, dqk), kctx),
                  pl.BlockSpec((seq, dve), klat),
                  pl.BlockSpec((ctx, dve), kctx)],
        out_specs=pl.BlockSpec((tq, dv), qmap),
        out_shape=jax.ShapeDtypeStruct((n_batch * seq, n_heads * dv), BF16),
        scratch_shapes=[pltpu.VMEM((tq, seq + ctx), F32), pltpu.VMEM((tq, seq + ctx), BF16)],
        compiler_params=_cparams(("parallel", "parallel", "arbitrary")),
        name="attention",
    )(q, k, k, v, v)
    if not with_ctx_queries:
        return out, None

    out_ctx = pl.pallas_call(
        _attn_ctx_kernel,
        grid=(n_batch, n_heads),
        in_specs=[pl.BlockSpec((ctx, dqk), lambda b, h: (ctx_block0 + b, h)),
                  pl.BlockSpec((ctx, dqk), lambda b, h: (ctx_block0 + b, h // group)),
                  pl.BlockSpec((ctx, dve), lambda b, h: (ctx_block0 + b, h // group))],
        out_specs=pl.BlockSpec((ctx, dv), lambda b, h: (b, h)),
        out_shape=jax.ShapeDtypeStruct((n_batch * ctx, n_heads * dv), BF16),
        compiler_params=_cparams(("parallel", "parallel")),
        name="attention_ctx",
    )(q, k, v)
    return out, out_ctx


def _post_kernel(*refs, n_lat_tiles, has_ctx):
    attn = refs[0][...]
    if has_ctx:
        attn = jnp.where(pl.program_id(0) >= n_lat_tiles, refs[1][...], attn)
    (x_ref, mod_ref, wo_ref, g_ref, wq_ref, k1_ref, k2_ref,
     x1_out, h_out, s1_out, s2_out) = refs[2 if has_ctx else 1:]
    y = jnp.dot(attn, wo_ref[...], preferred_element_type=F32)
    x1 = x_ref[...] + mod_ref[0, 2] * y
    x1_out[...] = x1
    h = _rmsn(x1, g_ref[...])
    hb = (h * (1.0 + mod_ref[0, 4]) + mod_ref[0, 3]).astype(BF16)
    h_out[...] = hb
    qp = jnp.dot(hb, wq_ref[...], preferred_element_type=F32).astype(BF16)
    hw = 2 * PEER_HALF
    for hd in range(PEER_HEADS):
        s1_out[hd] = lax.dot_general(k1_ref[hd], qp[:, hd * hw:hd * hw + PEER_HALF], _NT,
                                     preferred_element_type=F32)
        s2_out[hd] = lax.dot_general(k2_ref[hd], qp[:, hd * hw + PEER_HALF:(hd + 1) * hw], _NT,
                                     preferred_element_type=F32)


def _post_attention(attn, attn_ctx, xa, mod, w_o, gain, w_query, k1, k2, n_rows, geom):
    d = xa.shape[1]
    tm = ROW_TILE
    row, const, modmap, _ = _row_maps(*geom)
    n_lat_tiles = geom[0]
    kern = functools.partial(_post_kernel, n_lat_tiles=n_lat_tiles, has_ctx=attn_ctx is not None)
    attn_specs = [pl.BlockSpec((tm, attn.shape[1]), lambda i: (jnp.minimum(i, n_lat_tiles - 1), 0))]
    attn_args = [attn]
    if attn_ctx is not None:
        attn_specs.append(pl.BlockSpec((tm, attn.shape[1]),
                                       lambda i: (jnp.maximum(i - n_lat_tiles, 0), 0)))
        attn_args.append(attn_ctx)

    def const3(i):
        return (0, 0, 0)

    def smap(i):
        return (0, 0, i)

    sshape = jax.ShapeDtypeStruct((PEER_HEADS, PEER_KEYS, n_rows), F32)
    return pl.pallas_call(
        kern,
        grid=(n_rows // tm,),
        in_specs=attn_specs + [
            pl.BlockSpec((tm, d), row),
            pl.BlockSpec((1, N_MOD, 1, d), modmap),
            pl.BlockSpec(w_o.shape, const),
            pl.BlockSpec((1, d), const),
            pl.BlockSpec(w_query.shape, const),
            pl.BlockSpec(k1.shape, const3),
            pl.BlockSpec(k2.shape, const3)],
        out_specs=[pl.BlockSpec((tm, d), row),
                   pl.BlockSpec((tm, d), row),
                   pl.BlockSpec((PEER_HEADS, PEER_KEYS, tm), smap),
                   pl.BlockSpec((PEER_HEADS, PEER_KEYS, tm), smap)],
        out_shape=[jax.ShapeDtypeStruct((n_rows, d), F32),
                   jax.ShapeDtypeStruct((n_rows, d), BF16),
                   sshape, sshape],
        compiler_params=_cparams(("parallel",)),
        name="post_attention",
    )(*attn_args, xa, mod, w_o, gain.reshape(1, d), w_query, k1, k2)


def _cmpx(v, i, j):
    a, b = v[i], v[j]
    if b is None:
        return
    if a is None:
        v[i], v[j] = b, None
        return
    v[i], v[j] = jnp.maximum(a, b), jnp.minimum(a, b)


def _bitonic_sort_desc(v):
    n = len(v)
    k = 2
    while k <= n:
        j = k // 2
        while j >= 1:
            for i in range(n):
                l = i ^ j
                if l > i:
                    if (i & k) == 0:
                        _cmpx(v, i, l)
                    else:
                        _cmpx(v, l, i)
            j //= 2
        k *= 2
    return v


def _bitonic_merge_desc(v):
    n = len(v)
    j = n // 2
    while j >= 1:
        for i in range(n):
            l = i ^ j
            if l > i:
                _cmpx(v, i, l)
        j //= 2
    return v


def _max_none(a, b):
    if a is None:
        return b
    if b is None:
        return a
    return jnp.maximum(a, b)


def _merge_top(x, y):
    n = len(x)
    return _bitonic_merge_desc([_max_none(x[i], y[n - 1 - i]) for i in range(n)])


def _top16_rows(s):
    v = [s[r * SUBLANES:(r + 1) * SUBLANES, :] for r in range(PEER_KEYS // SUBLANES)]
    v = _bitonic_sort_desc(v)
    for shift in (4, 2, 1):
        w = [pltpu.roll(x, shift, 0) for x in v]
        v = _merge_top(v, w)
    return v


def _route_kernel(s1_ref, s2_ref, cnt_out, e1_out, rank_out, e2_out):
    k = PEER_TOPK

    def head(hd, carry):
        s1 = s1_ref[hd]
        s2 = s2_ref[hd]
        a = _top16_rows(s1)
        b = _top16_rows(s2)
        cand = {(p, q): a[p] + b[q] for p in range(k) for q in range(k // (p + 1))}
        row0 = [cand[0, q] for q in range(k)]
        rest = [cand[p, q] for p in range(1, k) for q in range(k // (p + 1))]
        l1 = _bitonic_sort_desc(rest[:k])
        l2 = _bitonic_sort_desc(rest[k:2 * k])
        l3 = _bitonic_sort_desc(rest[2 * k:] + [None] * (3 * k - len(rest)))
        top = _merge_top(_merge_top(row0, l1), _merge_top(l2, l3))
        tau = top[k - 1]
        z = jnp.ones_like(tau)
        for t in top[1:]:
            z = z + jnp.exp(t - top[0])
        half_rz = 0.5 / z
        thr = []
        for q in range(k):
            t = jnp.full_like(tau, jnp.inf)
            for p in range(k // (q + 1)):
                t = jnp.where(cand[p, q] >= tau, a[p], t)
            thr.append(t)
        cnts, e1s, ranks, e2s = [], [], [], []
        for r in range(PEER_KEYS // SUBLANES):
            rows = slice(r * SUBLANES, (r + 1) * SUBLANES)
            s1v, s2v = s1[rows, :], s2[rows, :]
            cnt = jnp.zeros_like(s1v)
            rank = jnp.zeros_like(s2v)
            for q in range(k):
                cnt = jnp.where(s1v >= thr[q], q + 1.0, cnt)
                rank = jnp.where(b[q] > s2v, q + 1.0, rank)
            cnts.append(cnt)
            ranks.append(rank)
            e1s.append(jnp.exp(s1v - a[0]))
            e2s.append(jnp.exp(s2v - b[0]) * half_rz)
        cnt_out[hd] = jnp.concatenate(cnts, axis=0)
        e1_out[hd] = jnp.concatenate(e1s, axis=0)
        rank_out[hd] = jnp.concatenate(ranks, axis=0).astype(BF16)
        e2_out[hd] = jnp.concatenate(e2s, axis=0).astype(BF16)
        return carry

    lax.fori_loop(0, PEER_HEADS, head, 0)


def _route(s1t, s2t):
    n_rows = s1t.shape[2]
    tl = ROUTE_TL
    spec = pl.BlockSpec((PEER_HEADS, PEER_KEYS, tl), lambda i: (0, 0, i))
    f32s = jax.ShapeDtypeStruct(s1t.shape, F32)
    bf16s = jax.ShapeDtypeStruct(s1t.shape, BF16)
    return pl.pallas_call(
        _route_kernel,
        grid=(n_rows // tl,),
        in_specs=[spec, spec],
        out_specs=[spec, spec, spec, spec],
        out_shape=[f32s, f32s, bf16s, bf16s],
        compiler_params=_cparams(("parallel",)),
        name="peer_route",
    )(s1t, s2t)


def _expert_kernel(h_ref, u_ref, vt_ref, rank_ref, e2_ref, cnt_ref, e1_ref, x_ref, mod_ref, g_ref,
                   o_ref, acc_ref, gel_ref, ga_ref, *, final_norm):
    ei = pl.program_id(1)
    tm = h_ref.shape[0]

    @pl.when(ei == 0)
    def _():
        acc_ref[...] = jnp.zeros_like(acc_ref)

    h = h_ref[...]
    sqrt_half = math.sqrt(0.5)
    zero = jnp.zeros((BF16_ROWS, tm), BF16)
    n_rowgrp = PEER_KEYS // BF16_ROWS
    n_sub = EXP_TE // EXP_SUB

    def activations(sb):
        esl = slice(sb * EXP_SUB, (sb + 1) * EXP_SUB)
        act = lax.dot_general(u_ref[esl, :], h, _NT, preferred_element_type=F32)
        gel_ref[esl, :] = (act * (1.0 + lax.erf(act * sqrt_half))).astype(BF16)

    def bcast_rows(ref, hd, il):
        parts = [jnp.broadcast_to(ref[hd, il:il + 1, lb * LANES:(lb + 1) * LANES],
                                  (BF16_ROWS, LANES)).astype(BF16) for lb in range(tm // LANES)]
        return jnp.concatenate(parts, axis=1)

    def gated(sb):
        for il in range(sb * EXP_SUB // PEER_KEYS, (sb + 1) * EXP_SUB // PEER_KEYS):
            acc = [None] * n_rowgrp
            for hd in range(PEER_HEADS):
                cnt = bcast_rows(cnt_ref, hd, il)
                e1 = bcast_rows(e1_ref, hd, il)
                for r in range(n_rowgrp):
                    rows = slice(r * BF16_ROWS, (r + 1) * BF16_ROWS)
                    term = jnp.where(rank_ref[hd, rows, :] < cnt, e2_ref[hd, rows, :], zero) * e1
                    acc[r] = term if acc[r] is None else acc[r] + term
            for r in range(n_rowgrp):
                rows = slice(il * PEER_KEYS + r * BF16_ROWS, il * PEER_KEYS + (r + 1) * BF16_ROWS)
                ga_ref[rows, :] = acc[r] * gel_ref[rows, :]

    activations(0)
    for sb in range(n_sub):
        esl = slice(sb * EXP_SUB, (sb + 1) * EXP_SUB)
        if sb + 1 < n_sub:
            activations(sb + 1)
        gated(sb)
        acc_ref[...] += jnp.dot(vt_ref[:, esl], ga_ref[esl, :], preferred_element_type=F32)

    @pl.when(ei == pl.num_programs(1) - 1)
    def _():
        x2 = x_ref[...] + mod_ref[0, 5] * acc_ref[...].T
        if final_norm:
            x2 = _rmsn(x2, g_ref[...])
        o_ref[...] = x2


def _experts(hb, u, vt, rank, e2, cnt, e1, x1, mod, gain, n_rows, tiles_per_batch, n_batch,
             final_norm):
    d = x1.shape[1]
    tm, te = EXP_TM, EXP_TE
    n_exp = u.shape[0]
    grp = te // PEER_KEYS

    def tok(t, e):
        return (t, 0)

    def stok(t, e):
        return (0, 0, t)

    def sexp(t, e):
        return (0, e, t)

    return pl.pallas_call(
        functools.partial(_expert_kernel, final_norm=final_norm),
        grid=(n_rows // tm, n_exp // te),
        in_specs=[pl.BlockSpec((tm, d), tok),
                  pl.BlockSpec((te, d), lambda t, e: (e, 0)),
                  pl.BlockSpec((d, te), lambda t, e: (0, e)),
                  pl.BlockSpec((PEER_HEADS, PEER_KEYS, tm), stok),
                  pl.BlockSpec((PEER_HEADS, PEER_KEYS, tm), stok),
                  pl.BlockSpec((PEER_HEADS, grp, tm), sexp),
                  pl.BlockSpec((PEER_HEADS, grp, tm), sexp),
                  pl.BlockSpec((tm, d), tok),
                  pl.BlockSpec((1, N_MOD, 1, d),
                               lambda t, e: (jnp.minimum(t // tiles_per_batch, n_batch), 0, 0, 0)),
                  pl.BlockSpec((1, d), lambda t, e: (0, 0))],
        out_specs=pl.BlockSpec((tm, d), tok),
        out_shape=jax.ShapeDtypeStruct((n_rows, d), F32),
        scratch_shapes=[pltpu.VMEM((d, tm), F32), pltpu.VMEM((te, tm), BF16),
                        pltpu.VMEM((te, tm), BF16)],
        compiler_params=_cparams(("parallel", "arbitrary")),
        name="peer_experts",
    )(hb, u, vt, rank, e2, cnt, e1, x1, mod, gain.reshape(1, d))


def _peer(attn, attn_ctx, xa, mod, w_o, norm_ffn, w_query, k1, k2, u, v, gain_out, n_rows, geom,
          final_norm):
    x1, hb, s1t, s2t = _post_attention(attn, attn_ctx, xa, mod, w_o.astype(BF16), norm_ffn,
                                       w_query.astype(BF16), k1.astype(BF16), k2.astype(BF16),
                                       n_rows, geom)
    cnt, e1, rank, e2 = _route(s1t, s2t)
    return _experts(hb, u.astype(BF16), v.T.astype(BF16), rank, e2, cnt, e1, x1, mod, gain_out,
                    n_rows, geom[1], geom[2], final_norm)


def _prep_mla(w_in, w_uq, w_ukv):
    d = w_in.shape[0]
    w_in_p = jnp.concatenate([w_in, jnp.zeros((d, LANES - MLA_ROPE), w_in.dtype)], axis=1)
    w_uq_p = jnp.pad(w_uq.reshape(MLA_Q_RANK, MLA_HEADS, MLA_NOPE + MLA_ROPE),
                     ((0, 0), (0, 0), (0, 2 * LANES - MLA_NOPE - MLA_ROPE)))
    w_uq_p = w_uq_p.reshape(MLA_Q_RANK, MLA_HEADS * 2 * LANES)
    w_ukv_p = w_ukv.reshape(MLA_KV_RANK, MLA_HEADS, 2, MLA_NOPE).transpose(0, 2, 1, 3)
    w_ukv_p = w_ukv_p.reshape(MLA_KV_RANK, 2 * MLA_HEADS * MLA_NOPE)
    return w_in_p.astype(BF16), w_uq_p.astype(BF16), w_ukv_p.astype(BF16)


def kernel(x, c, ctx, c_ctx, l0_ada_w, l0_ada_b, l0_norm_mix, l0_norm_ffn, l0_mla_w_in, l0_mla_q_norm, l0_mla_w_uq, l0_mla_kv_norm, l0_mla_w_ukv, l0_mla_w_o, l0_peer_w_query, l0_peer_k1, l0_peer_k2, l0_peer_u, l0_peer_v, l1_ada_w, l1_ada_b, l1_norm_mix, l1_norm_ffn, l1_gqa_w_qkv, l1_gqa_q_norm, l1_gqa_k_norm, l1_gqa_w_o, l1_peer_w_query, l1_peer_k1, l1_peer_k2, l1_peer_u, l1_peer_v, norm_out):
    n_batch, seq, d = x.shape
    n_ctx = ctx.shape[1]
    assert seq % ATTN_TQ == 0 and seq % GRID_W == 0 and (n_batch * seq) % n_ctx == 0
    assert seq % ROW_TILE == 0 and (n_batch * n_ctx) % ROW_TILE == 0
    assert seq % EXP_TM == 0 and (n_batch * n_ctx) % EXP_TM == 0 and ROW_TILE == EXP_TM
    n_lat = n_batch * seq
    n_all = n_lat + n_batch * n_ctx
    geom = (n_lat // ROW_TILE, seq // ROW_TILE, n_batch, seq // ROW_TILE)

    xa = jnp.concatenate([x.reshape(n_lat, d), ctx.reshape(n_batch * n_ctx, d)], axis=0)
    cc = jnp.zeros((SUBLANES, d), F32).at[:n_batch].set(c).at[n_batch].set(c_ctx)
    mod0 = _modulation(cc, l0_ada_w, l0_ada_b).reshape(SUBLANES, N_MOD, 1, d)
    mod1 = _modulation(cc, l1_ada_w, l1_ada_b).reshape(SUBLANES, N_MOD, 1, d)

    w_in_p, w_uq_p, w_ukv_p = _prep_mla(l0_mla_w_in, l0_mla_w_uq, l0_mla_w_ukv)
    q, k, v = _mixer_proj(
        _mla_proj_kernel, xa, l0_norm_mix, mod0,
        [w_in_p, l0_mla_q_norm.reshape(1, -1), w_uq_p, l0_mla_kv_norm.reshape(1, -1), w_ukv_p],
        _rope_tables(seq, MLA_ROPE, ROW_TILE),
        [MLA_HEADS * 2 * LANES, MLA_HEADS * 2 * LANES, MLA_HEADS * 2 * MLA_V], geom, "mla_proj")
    attn, attn_ctx = _attention(q, k, v, n_batch, seq, n_ctx, MLA_HEADS, MLA_HEADS, 2 * LANES,
                                MLA_V, True)
    xa = _peer(attn, attn_ctx, xa, mod0, l0_mla_w_o, l0_norm_ffn, l0_peer_w_query, l0_peer_k1,
               l0_peer_k2, l0_peer_u, l0_peer_v, norm_out, n_all, geom, False)

    q, k, v = _mixer_proj(
        _gqa_proj_kernel, xa, l1_norm_mix, mod1,
        [l1_gqa_w_qkv.astype(BF16), l1_gqa_q_norm.reshape(1, -1), l1_gqa_k_norm.reshape(1, -1)],
        _rope_tables(seq, GQA_DIM, ROW_TILE),
        [GQA_HEADS * GQA_DIM, GQA_KV_HEADS * GQA_DIM, GQA_KV_HEADS * 2 * GQA_DIM], geom, "gqa_proj")
    attn, _ = _attention(q, k, v, n_batch, seq, n_ctx, GQA_HEADS, GQA_KV_HEADS, GQA_DIM, GQA_DIM,
                         False)
    out = _peer(attn, None, xa, mod1, l1_gqa_w_o, l1_norm_ffn, l1_peer_w_query, l1_peer_k1,
                l1_peer_k2, l1_peer_u, l1_peer_v, norm_out, n_lat, geom, True)
    return out.reshape(n_batch, seq, d)
```

```python
import functools
import math

import jax
import jax.numpy as jnp
from jax import lax
from jax.experimental import pallas as pl
from jax.experimental.pallas import tpu as pltpu

F32 = jnp.float32
BF16 = jnp.bfloat16

GRID_W = 64
NORM_EPS = 1e-6
ROPE_THETA = 10000.0
N_MOD = 6
MLA_HEADS = 8
MLA_Q_RANK = 384
MLA_KV_RANK = 256
MLA_NOPE = 128
MLA_ROPE = 64
MLA_V = 128
GQA_HEADS = 8
GQA_KV_HEADS = 2
GQA_DIM = 128
PEER_HEADS = 8
PEER_KEYS = 128
PEER_HALF = 128
PEER_TOPK = 16

LANES = 128
SUBLANES = 8
BF16_ROWS = 16
VMEM_LIMIT = 56 * 1024 * 1024

ROW_TILE = 512
ATTN_TQ = 1024
ATTN_CHAIN = 256
ROUTE_TL = 256
EXP_TE = 1024
EXP_ACT_TM = 1024
EXP_GATE_TM = 512
EXP_OUT_TM = 1024

_NT = (((1,), (1,)), ((), ()))
LOG2E = math.log2(math.e)


def _cparams(sem):
    return pltpu.CompilerParams(dimension_semantics=sem, vmem_limit_bytes=VMEM_LIMIT)


def _rmsn(x, g):
    ms = jnp.mean(x * x, axis=-1, keepdims=True)
    return x * lax.rsqrt(ms + NORM_EPS) * g


def _mod_kernel(c_ref, w_ref, b_ref, o_ref):
    s = jax.nn.silu(c_ref[...])
    o_ref[...] = jnp.dot(s, w_ref[...], preferred_element_type=F32,
                         precision=lax.Precision.HIGHEST) + b_ref[...]


def _modulation(cc, w, b):
    d, n = w.shape
    bn = n // 4
    return pl.pallas_call(
        _mod_kernel,
        grid=(n // bn,),
        in_specs=[pl.BlockSpec((SUBLANES, d), lambda j: (0, 0)),
                  pl.BlockSpec((d, bn), lambda j: (0, j)),
                  pl.BlockSpec((1, bn), lambda j: (0, j))],
        out_specs=pl.BlockSpec((SUBLANES, bn), lambda j: (0, j)),
        out_shape=jax.ShapeDtypeStruct((SUBLANES, n), F32),
        compiler_params=_cparams(("arbitrary",)),
        name="adaln_mod",
    )(cc, w, b.reshape(1, n))


def _rope_tables(seq, rot_dim, extra_rows):
    rows = seq // GRID_W
    row = jnp.repeat(jnp.arange(rows, dtype=F32), GRID_W)
    col = jnp.tile(jnp.arange(GRID_W, dtype=F32), rows)
    d_axis = rot_dim // 2
    inv_freq = ROPE_THETA ** (-jnp.arange(0, d_axis, 2, dtype=F32) / d_axis)
    ang_r = row[:, None] * inv_freq
    ang_c = col[:, None] * inv_freq
    ang = jnp.concatenate([ang_r, ang_r, ang_c, ang_c], axis=-1)
    cos, sin = jnp.cos(ang), jnp.sin(ang)
    pad = LANES - rot_dim
    cos = jnp.pad(cos, ((0, extra_rows), (0, pad)), constant_values=1.0)
    sin = jnp.pad(sin, ((0, extra_rows), (0, pad)))
    quarter = rot_dim // 4
    first = (jnp.arange(LANES) % (2 * quarter)) < quarter
    sin_a = jnp.where(first[None, :], -sin, 0.0)
    sin_b = jnp.where(first[None, :], 0.0, sin)
    return cos, sin_a, sin_b


def _rope(c, cos, sin_a, sin_b, quarter):
    return (c * cos + pltpu.roll(c, LANES - quarter, 1) * sin_a
            + pltpu.roll(c, quarter, 1) * sin_b)


def _mla_proj_kernel(x_ref, g_ref, mod_ref, win_ref, qn_ref, wuq_ref, kvn_ref, wukv_ref,
                     cos_ref, sa_ref, sb_ref, q_out, k_out, v_out):
    h = _rmsn(x_ref[...], g_ref[...])
    h = h * (1.0 + mod_ref[0, 1]) + mod_ref[0, 0]
    a = jnp.dot(h.astype(BF16), win_ref[...], preferred_element_type=F32)
    cq = _rmsn(a[:, :MLA_Q_RANK], qn_ref[...]).astype(BF16)
    ckv = _rmsn(a[:, MLA_Q_RANK:MLA_Q_RANK + MLA_KV_RANK], kvn_ref[...]).astype(BF16)
    cos, sa, sb = cos_ref[...], sa_ref[...], sb_ref[...]
    quarter = MLA_ROPE // 4
    kr = _rope(a[:, MLA_Q_RANK + MLA_KV_RANK:], cos, sa, sb, quarter).astype(BF16)
    q = jnp.dot(cq, wuq_ref[...], preferred_element_type=F32)
    kv = jnp.dot(ckv, wukv_ref[...], preferred_element_type=F32)
    scale = LOG2E / math.sqrt(MLA_NOPE + MLA_ROPE)
    hw = 2 * LANES
    ones = jnp.ones((q.shape[0], LANES), BF16)
    for hd in range(MLA_HEADS):
        q_out[:, hd * hw:hd * hw + LANES] = (q[:, hd * hw:hd * hw + LANES] * scale).astype(BF16)
        qr = _rope(q[:, hd * hw + LANES:(hd + 1) * hw], cos, sa, sb, quarter) * scale
        q_out[:, hd * hw + LANES:(hd + 1) * hw] = qr.astype(BF16)
        k_out[:, hd * hw:hd * hw + LANES] = kv[:, hd * LANES:(hd + 1) * LANES].astype(BF16)
        k_out[:, hd * hw + LANES:(hd + 1) * hw] = kr
        v0 = (MLA_HEADS + hd) * LANES
        v_out[:, hd * hw:hd * hw + LANES] = kv[:, v0:v0 + LANES].astype(BF16)
        v_out[:, hd * hw + LANES:(hd + 1) * hw] = ones


def _gqa_proj_kernel(x_ref, g_ref, mod_ref, wqkv_ref, qn_ref, kn_ref,
                     cos_ref, sa_ref, sb_ref, q_out, k_out, v_out):
    h = _rmsn(x_ref[...], g_ref[...])
    h = h * (1.0 + mod_ref[0, 1]) + mod_ref[0, 0]
    qkv = jnp.dot(h.astype(BF16), wqkv_ref[...], preferred_element_type=F32)
    cos, sa, sb = cos_ref[...], sa_ref[...], sb_ref[...]
    quarter = GQA_DIM // 4
    scale = LOG2E / math.sqrt(GQA_DIM)
    for hd in range(GQA_HEADS):
        c = _rmsn(qkv[:, hd * LANES:(hd + 1) * LANES], qn_ref[...])
        q_out[:, hd * LANES:(hd + 1) * LANES] = (_rope(c, cos, sa, sb, quarter) * scale).astype(BF16)
    k0 = GQA_HEADS * LANES
    for hd in range(GQA_KV_HEADS):
        c = _rmsn(qkv[:, k0 + hd * LANES:k0 + (hd + 1) * LANES], kn_ref[...])
        k_out[:, hd * LANES:(hd + 1) * LANES] = _rope(c, cos, sa, sb, quarter).astype(BF16)
    v0 = k0 + GQA_KV_HEADS * LANES
    ones = jnp.ones((qkv.shape[0], LANES), BF16)
    for hd in range(GQA_KV_HEADS):
        v_out[:, 2 * hd * LANES:(2 * hd + 1) * LANES] = (
            qkv[:, v0 + hd * LANES:v0 + (hd + 1) * LANES].astype(BF16))
        v_out[:, (2 * hd + 1) * LANES:(2 * hd + 2) * LANES] = ones


def _row_maps(n_lat_tiles, tiles_per_batch, n_batch, tiles_per_seq):
    def row(i):
        return (i, 0)

    def const(i):
        return (0, 0)

    def mod(i):
        return (jnp.minimum(i // tiles_per_batch, n_batch), 0, 0, 0)

    def rope(i):
        return (jnp.where(i < n_lat_tiles, i % tiles_per_seq, tiles_per_seq), 0)

    return row, const, mod, rope


def _mixer_proj(kern, xa, gain, mod, weights, tables, out_widths, geom, name):
    n_rows, d = xa.shape
    tm = ROW_TILE
    row, const, modmap, ropemap = _row_maps(*geom)
    in_specs = [pl.BlockSpec((tm, d), row),
                pl.BlockSpec((1, d), const),
                pl.BlockSpec((1, N_MOD, 1, d), modmap)]
    in_specs += [pl.BlockSpec(w.shape, const) for w in weights]
    in_specs += [pl.BlockSpec((tm, LANES), ropemap) for _ in tables]
    return pl.pallas_call(
        kern,
        grid=(n_rows // tm,),
        in_specs=in_specs,
        out_specs=[pl.BlockSpec((tm, w), row) for w in out_widths],
        out_shape=[jax.ShapeDtypeStruct((n_rows, w), BF16) for w in out_widths],
        compiler_params=_cparams(("parallel",)),
        name=name,
    )(xa, gain.reshape(1, d), mod, *weights, *tables)


def _attn_kernel(q_ref, kl_ref, kc_ref, vl_ref, vc_ref, o_ref, s_ref, p_ref):
    n_lat = kl_ref.shape[0]
    dv = o_ref.shape[1]
    for c in range(ATTN_TQ // ATTN_CHAIN):
        rows = slice(c * ATTN_CHAIN, (c + 1) * ATTN_CHAIN)
        q = q_ref[rows, :]
        s_ref[rows, :n_lat] = lax.dot_general(q, kl_ref[...], _NT, preferred_element_type=F32)
        s_ref[rows, n_lat:] = lax.dot_general(q, kc_ref[...], _NT, preferred_element_type=F32)
        s = s_ref[rows, :]
        p_ref[rows, :] = jnp.exp2(s - s.max(axis=-1, keepdims=True)).astype(BF16)
        o = (jnp.dot(p_ref[rows, :n_lat], vl_ref[...], preferred_element_type=F32)
             + jnp.dot(p_ref[rows, n_lat:], vc_ref[...], preferred_element_type=F32))
        o_ref[rows, :] = (o[:, :dv] / o[:, dv:]).astype(BF16)


def _attn_ctx_kernel(q_ref, kc_ref, vc_ref, o_ref):
    dv = o_ref.shape[1]
    s = lax.dot_general(q_ref[...], kc_ref[...], _NT, preferred_element_type=F32)
    p = jnp.exp2(s - s.max(axis=-1, keepdims=True)).astype(BF16)
    o = jnp.dot(p, vc_ref[...], preferred_element_type=F32)
    o_ref[...] = (o[:, :dv] / o[:, dv:]).astype(BF16)


def _attention(q, k, v, n_batch, seq, ctx, n_heads, n_kv_heads, dqk, dv, with_ctx_queries):
    tq = ATTN_TQ
    nq = seq // tq
    group = n_heads // n_kv_heads
    ctx_block0 = n_batch * seq // ctx
    dve = 2 * dv

    def qmap(b, h, i):
        return (b * nq + i, h)

    def klat(b, h, i):
        return (b, h // group)

    def kctx(b, h, i):
        return (ctx_block0 + b, h // group)

    out = pl.pallas_call(
        _attn_kernel,
        grid=(n_batch, n_heads, nq),
        in_specs=[pl.BlockSpec((tq, dqk), qmap),
                  pl.BlockSpec((seq, dqk), klat),
                  pl.BlockSpec((ctx, dqk), kctx),
                  pl.BlockSpec((seq, dve), klat),
                  pl.BlockSpec((ctx, dve), kctx)],
        out_specs=pl.BlockSpec((tq, dv), qmap),
        out_shape=jax.ShapeDtypeStruct((n_batch * seq, n_heads * dv), BF16),
        scratch_shapes=[pltpu.VMEM((tq, seq + ctx), F32), pltpu.VMEM((tq, seq + ctx), BF16)],
        compiler_params=_cparams(("parallel", "parallel", "arbitrary")),
        name="attention",
    )(q, k, k, v, v)
    if not with_ctx_queries:
        return out, None

    out_ctx = pl.pallas_call(
        _attn_ctx_kernel,
        grid=(n_batch, n_heads),
        in_specs=[pl.BlockSpec((ctx, dqk), lambda b, h: (ctx_block0 + b, h)),
                  pl.BlockSpec((ctx, dqk), lambda b, h: (ctx_block0 + b, h // group)),
                  pl.BlockSpec((ctx, dve), lambda b, h: (ctx_block0 + b, h // group))],
        out_specs=pl.BlockSpec((ctx, dv), lambda b, h: (b, h)),
        out_shape=jax.ShapeDtypeStruct((n_batch * ctx, n_heads * dv), BF16),
        compiler_params=_cparams(("parallel", "parallel")),
        name="attention_ctx",
    )(q, k, v)
    return out, out_ctx


def _post_kernel(*refs, n_lat_tiles, has_ctx):
    attn = refs[0][...]
    if has_ctx:
        attn = jnp.where(pl.program_id(0) >= n_lat_tiles, refs[1][...], attn)
    (x_ref, mod_ref, wo_ref, g_ref, wq_ref, k1_ref, k2_ref,
     x1_out, h_out, s1_out, s2_out) = refs[2 if has_ctx else 1:]
    y = jnp.dot(attn, wo_ref[...], preferred_element_type=F32)
    x1 = x_ref[...] + mod_ref[0, 2] * y
    x1_out[...] = x1
    h = _rmsn(x1, g_ref[...])
    hb = (h * (1.0 + mod_ref[0, 4]) + mod_ref[0, 3]).astype(BF16)
    h_out[...] = hb
    qp = jnp.dot(hb, wq_ref[...], preferred_element_type=F32).astype(BF16)
    hw = 2 * PEER_HALF
    for hd in range(PEER_HEADS):
        s1_out[hd] = lax.dot_general(k1_ref[hd], qp[:, hd * hw:hd * hw + PEER_HALF], _NT,
                                     preferred_element_type=F32)
        s2_out[hd] = lax.dot_general(k2_ref[hd], qp[:, hd * hw + PEER_HALF:(hd + 1) * hw], _NT,
                                     preferred_element_type=F32)


def _post_attention(attn, attn_ctx, xa, mod, w_o, gain, w_query, k1, k2, n_rows, geom):
    d = xa.shape[1]
    tm = ROW_TILE
    row, const, modmap, _ = _row_maps(*geom)
    n_lat_tiles = geom[0]
    kern = functools.partial(_post_kernel, n_lat_tiles=n_lat_tiles, has_ctx=attn_ctx is not None)
    attn_specs = [pl.BlockSpec((tm, attn.shape[1]), lambda i: (jnp.minimum(i, n_lat_tiles - 1), 0))]
    attn_args = [attn]
    if attn_ctx is not None:
        attn_specs.append(pl.BlockSpec((tm, attn.shape[1]),
                                       lambda i: (jnp.maximum(i - n_lat_tiles, 0), 0)))
        attn_args.append(attn_ctx)

    def const3(i):
        return (0, 0, 0)

    def smap(i):
        return (0, 0, i)

    sshape = jax.ShapeDtypeStruct((PEER_HEADS, PEER_KEYS, n_rows), F32)
    return pl.pallas_call(
        kern,
        grid=(n_rows // tm,),
        in_specs=attn_specs + [
            pl.BlockSpec((tm, d), row),
            pl.BlockSpec((1, N_MOD, 1, d), modmap),
            pl.BlockSpec(w_o.shape, const),
            pl.BlockSpec((1, d), const),
            pl.BlockSpec(w_query.shape, const),
            pl.BlockSpec(k1.shape, const3),
            pl.BlockSpec(k2.shape, const3)],
        out_specs=[pl.BlockSpec((tm, d), row),
                   pl.BlockSpec((tm, d), row),
                   pl.BlockSpec((PEER_HEADS, PEER_KEYS, tm), smap),
                   pl.BlockSpec((PEER_HEADS, PEER_KEYS, tm), smap)],
        out_shape=[jax.ShapeDtypeStruct((n_rows, d), F32),
                   jax.ShapeDtypeStruct((n_rows, d), BF16),
                   sshape, sshape],
        compiler_params=_cparams(("parallel",)),
        name="post_attention",
    )(*attn_args, xa, mod, w_o, gain.reshape(1, d), w_query, k1, k2)


def _cmpx(v, i, j):
    a, b = v[i], v[j]
    if b is None:
        return
    if a is None:
        v[i], v[j] = b, None
        return
    v[i], v[j] = jnp.maximum(a, b), jnp.minimum(a, b)


def _bitonic_sort_desc(v):
    n = len(v)
    k = 2
    while k <= n:
        j = k // 2
        while j >= 1:
            for i in range(n):
                l = i ^ j
                if l > i:
                    if (i & k) == 0:
                        _cmpx(v, i, l)
                    else:
                        _cmpx(v, l, i)
            j //= 2
        k *= 2
    return v


def _bitonic_merge_desc(v):
    n = len(v)
    j = n // 2
    while j >= 1:
        for i in range(n):
            l = i ^ j
            if l > i:
                _cmpx(v, i, l)
        j //= 2
    return v


def _max_none(a, b):
    if a is None:
        return b
    if b is None:
        return a
    return jnp.maximum(a, b)


def _merge_top(x, y):
    n = len(x)
    return _bitonic_merge_desc([_max_none(x[i], y[n - 1 - i]) for i in range(n)])


def _top16_rows(s):
    v = [s[r * SUBLANES:(r + 1) * SUBLANES, :] for r in range(PEER_KEYS // SUBLANES)]
    v = _bitonic_sort_desc(v)
    for shift in (4, 2, 1):
        w = [pltpu.roll(x, shift, 0) for x in v]
        v = _merge_top(v, w)
    return v


def _route_kernel(s1_ref, s2_ref, cnt_out, e1_out, rank_out, e2_out):
    k = PEER_TOPK

    def head(hd, carry):
        s1 = s1_ref[hd]
        s2 = s2_ref[hd]
        a = _top16_rows(s1)
        b = _top16_rows(s2)
        cand = {(p, q): a[p] + b[q] for p in range(k) for q in range(k // (p + 1))}
        row0 = [cand[0, q] for q in range(k)]
        rest = [cand[p, q] for p in range(1, k) for q in range(k // (p + 1))]
        l1 = _bitonic_sort_desc(rest[:k])
        l2 = _bitonic_sort_desc(rest[k:2 * k])
        l3 = _bitonic_sort_desc(rest[2 * k:] + [None] * (3 * k - len(rest)))
        top = _merge_top(_merge_top(row0, l1), _merge_top(l2, l3))
        tau = top[k - 1]
        z = jnp.ones_like(tau)
        for t in top[1:]:
            z = z + jnp.exp(t - top[0])
        half_rz = 0.5 / z
        thr = []
        for q in range(k):
            t = jnp.full_like(tau, jnp.inf)
            for p in range(k // (q + 1)):
                t = jnp.where(cand[p, q] >= tau, a[p], t)
            thr.append(t)
        cnts, e1s, ranks, e2s = [], [], [], []
        for r in range(PEER_KEYS // SUBLANES):
            rows = slice(r * SUBLANES, (r + 1) * SUBLANES)
            s1v, s2v = s1[rows, :], s2[rows, :]
            cnt = jnp.zeros_like(s1v)
            rank = jnp.zeros_like(s2v)
            for q in range(k):
                cnt = jnp.where(s1v >= thr[q], q + 1.0, cnt)
                rank = jnp.where(b[q] > s2v, q + 1.0, rank)
            cnts.append(cnt)
            ranks.append(rank)
            e1s.append(jnp.exp(s1v - a[0]))
            e2s.append(jnp.exp(s2v - b[0]) * half_rz)
        cnt_out[hd] = jnp.concatenate(cnts, axis=0)
        e1_out[hd] = jnp.concatenate(e1s, axis=0)
        rank_out[hd] = jnp.concatenate(ranks, axis=0).astype(BF16)
        e2_out[hd] = jnp.concatenate(e2s, axis=0).astype(BF16)
        return carry

    lax.fori_loop(0, PEER_HEADS, head, 0)


def _route(s1t, s2t):
    n_rows = s1t.shape[2]
    tl = ROUTE_TL
    spec = pl.BlockSpec((PEER_HEADS, PEER_KEYS, tl), lambda i: (0, 0, i))
    f32s = jax.ShapeDtypeStruct(s1t.shape, F32)
    bf16s = jax.ShapeDtypeStruct(s1t.shape, BF16)
    return pl.pallas_call(
        _route_kernel,
        grid=(n_rows // tl,),
        in_specs=[spec, spec],
        out_specs=[spec, spec, spec, spec],
        out_shape=[f32s, f32s, bf16s, bf16s],
        compiler_params=_cparams(("parallel",)),
        name="peer_route",
    )(s1t, s2t)


def _act_kernel(h_ref, u_ref, gel_out):
    act = lax.dot_general(u_ref[...], h_ref[...], _NT, preferred_element_type=F32).astype(BF16)
    gel_out[...] = act * (1.0 + lax.erf(act * math.sqrt(0.5)))


def _gate_kernel(rank_ref, e2_ref, cnt_ref, e1_ref, gel_ref, ga_out):
    tm = gel_ref.shape[1]
    zero = jnp.zeros((BF16_ROWS, tm), BF16)
    n_rowgrp = PEER_KEYS // BF16_ROWS

    def bcast_rows(ref, hd, il):
        parts = [jnp.broadcast_to(ref[hd, il:il + 1, lb * LANES:(lb + 1) * LANES],
                                  (BF16_ROWS, LANES)).astype(BF16) for lb in range(tm // LANES)]
        return jnp.concatenate(parts, axis=1)

    for il in range(ga_out.shape[0] // PEER_KEYS):
        acc = [None] * n_rowgrp
        for hd in range(PEER_HEADS):
            cnt = bcast_rows(cnt_ref, hd, il)
            e1 = bcast_rows(e1_ref, hd, il)
            for r in range(n_rowgrp):
                rows = slice(r * BF16_ROWS, (r + 1) * BF16_ROWS)
                term = jnp.where(rank_ref[hd, rows, :] < cnt, e2_ref[hd, rows, :], zero) * e1
                acc[r] = term if acc[r] is None else acc[r] + term
        for r in range(n_rowgrp):
            rows = slice(il * PEER_KEYS + r * BF16_ROWS, il * PEER_KEYS + (r + 1) * BF16_ROWS)
            ga_out[rows, :] = acc[r] * gel_ref[rows, :]


def _combine_kernel(ga_ref, v_ref, x_ref, mod_ref, g_ref, o_ref, acc_ref, *, final_norm):
    ei = pl.program_id(1)

    @pl.when(ei == 0)
    def _():
        acc_ref[...] = jnp.zeros_like(acc_ref)

    acc_ref[...] += lax.dot_general(ga_ref[...], v_ref[...], (((0,), (0,)), ((), ())),
                                    preferred_element_type=F32)

    @pl.when(ei == pl.num_programs(1) - 1)
    def _():
        x2 = x_ref[...] + mod_ref[0, 5] * acc_ref[...]
        if final_norm:
            x2 = _rmsn(x2, g_ref[...])
        o_ref[...] = x2


def _experts(hb, u, v, rank, e2, cnt, e1, x1, mod, gain, n_rows, tiles_per_batch, n_batch,
             final_norm):
    d = x1.shape[1]
    n_exp = u.shape[0]
    te = EXP_TE
    grp = te // PEER_KEYS

    tm = EXP_ACT_TM
    gel = pl.pallas_call(
        _act_kernel,
        grid=(n_rows // tm, n_exp // te),
        in_specs=[pl.BlockSpec((tm, d), lambda t, e: (t, 0)),
                  pl.BlockSpec((te, d), lambda t, e: (e, 0))],
        out_specs=pl.BlockSpec((te, tm), lambda t, e: (e, t)),
        out_shape=jax.ShapeDtypeStruct((n_exp, n_rows), BF16),
        compiler_params=_cparams(("parallel", "parallel")),
        name="peer_act",
    )(hb, u)

    tm = EXP_GATE_TM
    ga = pl.pallas_call(
        _gate_kernel,
        grid=(n_rows // tm, n_exp // te),
        in_specs=[pl.BlockSpec((PEER_HEADS, PEER_KEYS, tm), lambda t, e: (0, 0, t)),
                  pl.BlockSpec((PEER_HEADS, PEER_KEYS, tm), lambda t, e: (0, 0, t)),
                  pl.BlockSpec((PEER_HEADS, grp, tm), lambda t, e: (0, e, t)),
                  pl.BlockSpec((PEER_HEADS, grp, tm), lambda t, e: (0, e, t)),
                  pl.BlockSpec((te, tm), lambda t, e: (e, t))],
        out_specs=pl.BlockSpec((te, tm), lambda t, e: (e, t)),
        out_shape=jax.ShapeDtypeStruct((n_exp, n_rows), BF16),
        compiler_params=_cparams(("parallel", "parallel")),
        name="peer_gate",
    )(rank, e2, cnt, e1, gel)

    tm = EXP_OUT_TM
    tpb = tiles_per_batch * ROW_TILE // tm
    return pl.pallas_call(
        functools.partial(_combine_kernel, final_norm=final_norm),
        grid=(n_rows // tm, n_exp // te),
        in_specs=[pl.BlockSpec((te, tm), lambda t, e: (e, t)),
                  pl.BlockSpec((te, d), lambda t, e: (e, 0)),
                  pl.BlockSpec((tm, d), lambda t, e: (t, 0)),
                  pl.BlockSpec((1, N_MOD, 1, d),
                               lambda t, e: (jnp.minimum(t // tpb, n_batch), 0, 0, 0)),
                  pl.BlockSpec((1, d), lambda t, e: (0, 0))],
        out_specs=pl.BlockSpec((tm, d), lambda t, e: (t, 0)),
        out_shape=jax.ShapeDtypeStruct((n_rows, d), F32),
        scratch_shapes=[pltpu.VMEM((tm, d), F32)],
        compiler_params=_cparams(("parallel", "arbitrary")),
        name="peer_combine",
    )(ga, v, x1, mod, gain.reshape(1, d))


def _peer(attn, attn_ctx, xa, mod, w_o, norm_ffn, w_query, k1, k2, u, v, gain_out, n_rows, geom,
          final_norm):
    x1, hb, s1t, s2t = _post_attention(attn, attn_ctx, xa, mod, w_o.astype(BF16), norm_ffn,
                                       w_query.astype(BF16), k1.astype(BF16), k2.astype(BF16),
                                       n_rows, geom)
    cnt, e1, rank, e2 = _route(s1t, s2t)
    return _experts(hb, u.astype(BF16), v.astype(BF16), rank, e2, cnt, e1, x1, mod, gain_out,
                    n_rows, geom[1], geom[2], final_norm)


def _prep_mla(w_in, w_uq, w_ukv):
    d = w_in.shape[0]
    w_in_p = jnp.concatenate([w_in, jnp.zeros((d, LANES - MLA_ROPE), w_in.dtype)], axis=1)
    w_uq_p = jnp.pad(w_uq.reshape(MLA_Q_RANK, MLA_HEADS, MLA_NOPE + MLA_ROPE),
                     ((0, 0), (0, 0), (0, 2 * LANES - MLA_NOPE - MLA_ROPE)))
    w_uq_p = w_uq_p.reshape(MLA_Q_RANK, MLA_HEADS * 2 * LANES)
    w_ukv_p = w_ukv.reshape(MLA_KV_RANK, MLA_HEADS, 2, MLA_NOPE).transpose(0, 2, 1, 3)
    w_ukv_p = w_ukv_p.reshape(MLA_KV_RANK, 2 * MLA_HEADS * MLA_NOPE)
    return w_in_p.astype(BF16), w_uq_p.astype(BF16), w_ukv_p.astype(BF16)


def kernel(x, c, ctx, c_ctx, l0_ada_w, l0_ada_b, l0_norm_mix, l0_norm_ffn, l0_mla_w_in, l0_mla_q_norm, l0_mla_w_uq, l0_mla_kv_norm, l0_mla_w_ukv, l0_mla_w_o, l0_peer_w_query, l0_peer_k1, l0_peer_k2, l0_peer_u, l0_peer_v, l1_ada_w, l1_ada_b, l1_norm_mix, l1_norm_ffn, l1_gqa_w_qkv, l1_gqa_q_norm, l1_gqa_k_norm, l1_gqa_w_o, l1_peer_w_query, l1_peer_k1, l1_peer_k2, l1_peer_u, l1_peer_v, norm_out):
    n_batch, seq, d = x.shape
    n_ctx = ctx.shape[1]
    assert seq % ATTN_TQ == 0 and seq % GRID_W == 0 and (n_batch * seq) % n_ctx == 0
    assert seq % ROW_TILE == 0 and (n_batch * n_ctx) % ROW_TILE == 0
    for tm in (EXP_ACT_TM, EXP_GATE_TM, EXP_OUT_TM):
        assert seq % tm == 0 and (n_batch * n_ctx) % tm == 0
    n_lat = n_batch * seq
    n_all = n_lat + n_batch * n_ctx
    geom = (n_lat // ROW_TILE, seq // ROW_TILE, n_batch, seq // ROW_TILE)

    xa = jnp.concatenate([x.reshape(n_lat, d), ctx.reshape(n_batch * n_ctx, d)], axis=0)
    cc = jnp.zeros((SUBLANES, d), F32).at[:n_batch].set(c).at[n_batch].set(c_ctx)
    mod0 = _modulation(cc, l0_ada_w, l0_ada_b).reshape(SUBLANES, N_MOD, 1, d)
    mod1 = _modulation(cc, l1_ada_w, l1_ada_b).reshape(SUBLANES, N_MOD, 1, d)

    w_in_p, w_uq_p, w_ukv_p = _prep_mla(l0_mla_w_in, l0_mla_w_uq, l0_mla_w_ukv)
    q, k, v = _mixer_proj(
        _mla_proj_kernel, xa, l0_norm_mix, mod0,
        [w_in_p, l0_mla_q_norm.reshape(1, -1), w_uq_p, l0_mla_kv_norm.reshape(1, -1), w_ukv_p],
        _rope_tables(seq, MLA_ROPE, ROW_TILE),
        [MLA_HEADS * 2 * LANES, MLA_HEADS * 2 * LANES, MLA_HEADS * 2 * MLA_V], geom, "mla_proj")
    attn, attn_ctx = _attention(q, k, v, n_batch, seq, n_ctx, MLA_HEADS, MLA_HEADS, 2 * LANES,
                                MLA_V, True)
    xa = _peer(attn, attn_ctx, xa, mod0, l0_mla_w_o, l0_norm_ffn, l0_peer_w_query, l0_peer_k1,
               l0_peer_k2, l0_peer_u, l0_peer_v, norm_out, n_all, geom, False)

    q, k, v = _mixer_proj(
        _gqa_proj_kernel, xa, l1_norm_mix, mod1,
        [l1_gqa_w_qkv.astype(BF16), l1_gqa_q_norm.reshape(1, -1), l1_gqa_k_norm.reshape(1, -1)],
        _rope_tables(seq, GQA_DIM, ROW_TILE),
        [GQA_HEADS * GQA_DIM, GQA_KV_HEADS * GQA_DIM, GQA_KV_HEADS * 2 * GQA_DIM], geom, "gqa_proj")
    attn, _ = _attention(q, k, v, n_batch, seq, n_ctx, GQA_HEADS, GQA_KV_HEADS, GQA_DIM, GQA_DIM,
                         False)
    out = _peer(attn, None, xa, mod1, l1_gqa_w_o, l1_norm_ffn, l1_peer_w_query, l1_peer_k1,
                l1_peer_k2, l1_peer_u, l1_peer_v, norm_out, n_lat, geom, True)
    return out.reshape(n_batch, seq, d)
```

```python
import functools
import math

import jax
import jax.numpy as jnp
from jax import lax
from jax.experimental import pallas as pl
from jax.experimental.pallas import tpu as pltpu

F32 = jnp.float32
BF16 = jnp.bfloat16

GRID_W = 64
NORM_EPS = 1e-6
ROPE_THETA = 10000.0
N_MOD = 6
MLA_HEADS = 8
MLA_Q_RANK = 384
MLA_KV_RANK = 256
MLA_NOPE = 128
MLA_ROPE = 64
MLA_V = 128
GQA_HEADS = 8
GQA_KV_HEADS = 2
GQA_DIM = 128
PEER_HEADS = 8
PEER_KEYS = 128
PEER_HALF = 128
PEER_TOPK = 16

LANES = 128
SUBLANES = 8
BF16_ROWS = 16
MXU_DIM = 256
VMEM_LIMIT = 56 * 1024 * 1024

ROW_TILE = 512
ATTN_TQ = 2048
ATTN_CHAIN = 256
ATTN_SLOTS = 4
ROUTE_TL = 256
EXP_TM = 512
EXP_TE = 2048
EXP_SUB = MXU_DIM

_NT = (((1,), (1,)), ((), ()))
LOG2E = math.log2(math.e)


def _cparams(sem):
    return pltpu.CompilerParams(dimension_semantics=sem, vmem_limit_bytes=VMEM_LIMIT)


def _rmsn(x, g):
    ms = jnp.mean(x * x, axis=-1, keepdims=True)
    return x * lax.rsqrt(ms + NORM_EPS) * g


def _mod_kernel(c_ref, w_ref, b_ref, o_ref):
    s = jax.nn.silu(c_ref[...])
    o_ref[...] = jnp.dot(s, w_ref[...], preferred_element_type=F32,
                         precision=lax.Precision.HIGHEST) + b_ref[...]


def _modulation(cc, w, b):
    d, n = w.shape
    bn = n // 4
    return pl.pallas_call(
        _mod_kernel,
        grid=(n // bn,),
        in_specs=[pl.BlockSpec((SUBLANES, d), lambda j: (0, 0)),
                  pl.BlockSpec((d, bn), lambda j: (0, j)),
                  pl.BlockSpec((1, bn), lambda j: (0, j))],
        out_specs=pl.BlockSpec((SUBLANES, bn), lambda j: (0, j)),
        out_shape=jax.ShapeDtypeStruct((SUBLANES, n), F32),
        compiler_params=_cparams(("arbitrary",)),
        name="adaln_mod",
    )(cc, w, b.reshape(1, n))


def _rope_tables(seq, rot_dim, extra_rows):
    rows = seq // GRID_W
    row = jnp.repeat(jnp.arange(rows, dtype=F32), GRID_W)
    col = jnp.tile(jnp.arange(GRID_W, dtype=F32), rows)
    d_axis = rot_dim // 2
    inv_freq = ROPE_THETA ** (-jnp.arange(0, d_axis, 2, dtype=F32) / d_axis)
    ang_r = row[:, None] * inv_freq
    ang_c = col[:, None] * inv_freq
    ang = jnp.concatenate([ang_r, ang_r, ang_c, ang_c], axis=-1)
    cos, sin = jnp.cos(ang), jnp.sin(ang)
    pad = LANES - rot_dim
    cos = jnp.pad(cos, ((0, extra_rows), (0, pad)), constant_values=1.0)
    sin = jnp.pad(sin, ((0, extra_rows), (0, pad)))
    quarter = rot_dim // 4
    first = (jnp.arange(LANES) % (2 * quarter)) < quarter
    sin_a = jnp.where(first[None, :], -sin, 0.0)
    sin_b = jnp.where(first[None, :], 0.0, sin)
    return cos, sin_a, sin_b


def _rope(c, cos, sin_a, sin_b, quarter):
    return (c * cos + pltpu.roll(c, LANES - quarter, 1) * sin_a
            + pltpu.roll(c, quarter, 1) * sin_b)


def _mla_proj_kernel(x_ref, g_ref, mod_ref, win_ref, qn_ref, wuq_ref, kvn_ref, wukv_ref,
                     cos_ref, sa_ref, sb_ref, q_out, k_out, v_out):
    h = _rmsn(x_ref[...], g_ref[...])
    h = h * (1.0 + mod_ref[0, 1]) + mod_ref[0, 0]
    a = jnp.dot(h.astype(BF16), win_ref[...], preferred_element_type=F32)
    cq = _rmsn(a[:, :MLA_Q_RANK], qn_ref[...]).astype(BF16)
    ckv = _rmsn(a[:, MLA_Q_RANK:MLA_Q_RANK + MLA_KV_RANK], kvn_ref[...]).astype(BF16)
    cos, sa, sb = cos_ref[...], sa_ref[...], sb_ref[...]
    quarter = MLA_ROPE // 4
    kr = _rope(a[:, MLA_Q_RANK + MLA_KV_RANK:], cos, sa, sb, quarter).astype(BF16)
    q = jnp.dot(cq, wuq_ref[...], preferred_element_type=F32)
    kv = jnp.dot(ckv, wukv_ref[...], preferred_element_type=F32)
    scale = LOG2E / math.sqrt(MLA_NOPE + MLA_ROPE)
    hw = 2 * LANES
    ones = jnp.ones((q.shape[0], LANES), BF16)
    for hd in range(MLA_HEADS):
        q_out[:, hd * hw:hd * hw + LANES] = (q[:, hd * hw:hd * hw + LANES] * scale).astype(BF16)
        qr = _rope(q[:, hd * hw + LANES:(hd + 1) * hw], cos, sa, sb, quarter) * scale
        q_out[:, hd * hw + LANES:(hd + 1) * hw] = qr.astype(BF16)
        k_out[:, hd * hw:hd * hw + LANES] = kv[:, hd * LANES:(hd + 1) * LANES].astype(BF16)
        k_out[:, hd * hw + LANES:(hd + 1) * hw] = kr
        v0 = (MLA_HEADS + hd) * LANES
        v_out[:, hd * hw:hd * hw + LANES] = kv[:, v0:v0 + LANES].astype(BF16)
        v_out[:, hd * hw + LANES:(hd + 1) * hw] = ones


def _gqa_proj_kernel(x_ref, g_ref, mod_ref, wqkv_ref, qn_ref, kn_ref,
                     cos_ref, sa_ref, sb_ref, q_out, k_out, v_out):
    h = _rmsn(x_ref[...], g_ref[...])
    h = h * (1.0 + mod_ref[0, 1]) + mod_ref[0, 0]
    qkv = jnp.dot(h.astype(BF16), wqkv_ref[...], preferred_element_type=F32)
    cos, sa, sb = cos_ref[...], sa_ref[...], sb_ref[...]
    quarter = GQA_DIM // 4
    scale = LOG2E / math.sqrt(GQA_DIM)
    for hd in range(GQA_HEADS):
        c = _rmsn(qkv[:, hd * LANES:(hd + 1) * LANES], qn_ref[...])
        q_out[:, hd * LANES:(hd + 1) * LANES] = (_rope(c, cos, sa, sb, quarter) * scale).astype(BF16)
    k0 = GQA_HEADS * LANES
    for hd in range(GQA_KV_HEADS):
        c = _rmsn(qkv[:, k0 + hd * LANES:k0 + (hd + 1) * LANES], kn_ref[...])
        k_out[:, hd * LANES:(hd + 1) * LANES] = _rope(c, cos, sa, sb, quarter).astype(BF16)
    v0 = k0 + GQA_KV_HEADS * LANES
    ones = jnp.ones((qkv.shape[0], LANES), BF16)
    for hd in range(GQA_KV_HEADS):
        v_out[:, 2 * hd * LANES:(2 * hd + 1) * LANES] = (
            qkv[:, v0 + hd * LANES:v0 + (hd + 1) * LANES].astype(BF16))
        v_out[:, (2 * hd + 1) * LANES:(2 * hd + 2) * LANES] = ones


def _row_maps(n_lat_tiles, tiles_per_batch, n_batch, tiles_per_seq):
    def row(i):
        return (i, 0)

    def const(i):
        return (0, 0)

    def mod(i):
        return (jnp.minimum(i // tiles_per_batch, n_batch), 0, 0, 0)

    def rope(i):
        return (jnp.where(i < n_lat_tiles, i % tiles_per_seq, tiles_per_seq), 0)

    return row, const, mod, rope


def _mixer_proj(kern, xa, gain, mod, weights, tables, out_widths, geom, name):
    n_rows, d = xa.shape
    tm = ROW_TILE
    row, const, modmap, ropemap = _row_maps(*geom)
    in_specs = [pl.BlockSpec((tm, d), row),
                pl.BlockSpec((1, d), const),
                pl.BlockSpec((1, N_MOD, 1, d), modmap)]
    in_specs += [pl.BlockSpec(w.shape, const) for w in weights]
    in_specs += [pl.BlockSpec((tm, LANES), ropemap) for _ in tables]
    return pl.pallas_call(
        kern,
        grid=(n_rows // tm,),
        in_specs=in_specs,
        out_specs=[pl.BlockSpec((tm, w), row) for w in out_widths],
        out_shape=[jax.ShapeDtypeStruct((n_rows, w), BF16) for w in out_widths],
        compiler_params=_cparams(("parallel",)),
        name=name,
    )(xa, gain.reshape(1, d), mod, *weights, *tables)


def _attn_kernel(q_ref, kl_ref, kc_ref, vl_ref, vc_ref, o_ref, s_ref, p_ref):
    n_lat = kl_ref.shape[0]
    dv = o_ref.shape[1]
    for c in range(ATTN_TQ // ATTN_CHAIN):
        rows = slice(c * ATTN_CHAIN, (c + 1) * ATTN_CHAIN)
        slot = c % ATTN_SLOTS
        q = q_ref[rows, :]
        s_ref[slot, :, :n_lat] = lax.dot_general(q, kl_ref[...], _NT, preferred_element_type=F32)
        s_ref[slot, :, n_lat:] = lax.dot_general(q, kc_ref[...], _NT, preferred_element_type=F32)
        s = s_ref[slot]
        p_ref[slot] = jnp.exp2(s - s.max(axis=-1, keepdims=True)).astype(BF16)
        o = (jnp.dot(p_ref[slot, :, :n_lat], vl_ref[...], preferred_element_type=F32)
             + jnp.dot(p_ref[slot, :, n_lat:], vc_ref[...], preferred_element_type=F32))
        o_ref[rows, :] = (o[:, :dv] / o[:, dv:]).astype(BF16)


def _attn_ctx_kernel(q_ref, kc_ref, vc_ref, o_ref):
    dv = o_ref.shape[1]
    s = lax.dot_general(q_ref[...], kc_ref[...], _NT, preferred_element_type=F32)
    p = jnp.exp2(s - s.max(axis=-1, keepdims=True)).astype(BF16)
    o = jnp.dot(p, vc_ref[...], preferred_element_type=F32)
    o_ref[...] = (o[:, :dv] / o[:, dv:]).astype(BF16)


def _attention(q, k, v, n_batch, seq, ctx, n_heads, n_kv_heads, dqk, dv, with_ctx_queries):
    tq = ATTN_TQ
    nq = seq // tq
    group = n_heads // n_kv_heads
    ctx_block0 = n_batch * seq // ctx
    dve = 2 * dv

    def qmap(b, h, i):
        return (b * nq + i, h)

    def klat(b, h, i):
        return (b, h // group)

    def kctx(b, h, i):
        return (ctx_block0 + b, h // group)

    out = pl.pallas_call(
        _attn_kernel,
        grid=(n_batch, n_heads, nq),
        in_specs=[pl.BlockSpec((tq, dqk), qmap),
                  pl.BlockSpec((seq, dqk), klat),
                  pl.BlockSpec((ctx, dqk), kctx),
                  pl.BlockSpec((seq, dve), klat),
                  pl.BlockSpec((ctx, dve), kctx)],
        out_specs=pl.BlockSpec((tq, dv), qmap),
        out_shape=jax.ShapeDtypeStruct((n_batch * seq, n_heads * dv), BF16),
        scratch_shapes=[pltpu.VMEM((ATTN_SLOTS, ATTN_CHAIN, seq + ctx), F32),
                        pltpu.VMEM((ATTN_SLOTS, ATTN_CHAIN, seq + ctx), BF16)],
        compiler_params=_cparams(("parallel", "parallel", "arbitrary")),
        name="attention",
    )(q, k, k, v, v)
    if not with_ctx_queries:
        return out, None

    out_ctx = pl.pallas_call(
        _attn_ctx_kernel,
        grid=(n_batch, n_heads),
        in_specs=[pl.BlockSpec((ctx, dqk), lambda b, h: (ctx_block0 + b, h)),
                  pl.BlockSpec((ctx, dqk), lambda b, h: (ctx_block0 + b, h // group)),
                  pl.BlockSpec((ctx, dve), lambda b, h: (ctx_block0 + b, h // group))],
        out_specs=pl.BlockSpec((ctx, dv), lambda b, h: (b, h)),
        out_shape=jax.ShapeDtypeStruct((n_batch * ctx, n_heads * dv), BF16),
        compiler_params=_cparams(("parallel", "parallel")),
        name="attention_ctx",
    )(q, k, v)
    return out, out_ctx


def _post_kernel(*refs, n_lat_tiles, has_ctx):
    attn = refs[0][...]
    if has_ctx:
        attn = jnp.where(pl.program_id(0) >= n_lat_tiles, refs[1][...], attn)
    (x_ref, mod_ref, wo_ref, g_ref, wq_ref, k1_ref, k2_ref,
     x1_out, h_out, s1_out, s2_out) = refs[2 if has_ctx else 1:]
    y = jnp.dot(attn, wo_ref[...], preferred_element_type=F32)
    x1 = x_ref[...] + mod_ref[0, 2] * y
    x1_out[...] = x1
    h = _rmsn(x1, g_ref[...])
    hb = (h * (1.0 + mod_ref[0, 4]) + mod_ref[0, 3]).astype(BF16)
    h_out[...] = hb
    qp = jnp.dot(hb, wq_ref[...], preferred_element_type=F32).astype(BF16)
    hw = 2 * PEER_HALF
    for hd in range(PEER_HEADS):
        s1_out[hd] = lax.dot_general(k1_ref[hd], qp[:, hd * hw:hd * hw + PEER_HALF], _NT,
                                     preferred_element_type=F32)
        s2_out[hd] = lax.dot_general(k2_ref[hd], qp[:, hd * hw + PEER_HALF:(hd + 1) * hw], _NT,
                                     preferred_element_type=F32)


def _post_attention(attn, attn_ctx, xa, mod, w_o, gain, w_query, k1, k2, n_rows, geom):
    d = xa.shape[1]
    tm = ROW_TILE
    row, const, modmap, _ = _row_maps(*geom)
    n_lat_tiles = geom[0]
    kern = functools.partial(_post_kernel, n_lat_tiles=n_lat_tiles, has_ctx=attn_ctx is not None)
    attn_specs = [pl.BlockSpec((tm, attn.shape[1]), lambda i: (jnp.minimum(i, n_lat_tiles - 1), 0))]
    attn_args = [attn]
    if attn_ctx is not None:
        attn_specs.append(pl.BlockSpec((tm, attn.shape[1]),
                                       lambda i: (jnp.maximum(i - n_lat_tiles, 0), 0)))
        attn_args.append(attn_ctx)

    def const3(i):
        return (0, 0, 0)

    def smap(i):
        return (0, 0, i)

    sshape = jax.ShapeDtypeStruct((PEER_HEADS, PEER_KEYS, n_rows), F32)
    return pl.pallas_call(
        kern,
        grid=(n_rows // tm,),
        in_specs=attn_specs + [
            pl.BlockSpec((tm, d), row),
            pl.BlockSpec((1, N_MOD, 1, d), modmap),
            pl.BlockSpec(w_o.shape, const),
            pl.BlockSpec((1, d), const),
            pl.BlockSpec(w_query.shape, const),
            pl.BlockSpec(k1.shape, const3),
            pl.BlockSpec(k2.shape, const3)],
        out_specs=[pl.BlockSpec((tm, d), row),
                   pl.BlockSpec((tm, d), row),
                   pl.BlockSpec((PEER_HEADS, PEER_KEYS, tm), smap),
                   pl.BlockSpec((PEER_HEADS, PEER_KEYS, tm), smap)],
        out_shape=[jax.ShapeDtypeStruct((n_rows, d), F32),
                   jax.ShapeDtypeStruct((n_rows, d), BF16),
                   sshape, sshape],
        compiler_params=_cparams(("parallel",)),
        name="post_attention",
    )(*attn_args, xa, mod, w_o, gain.reshape(1, d), w_query, k1, k2)


def _cmpx(v, i, j):
    a, b = v[i], v[j]
    if b is None:
        return
    if a is None:
        v[i], v[j] = b, None
        return
    v[i], v[j] = jnp.maximum(a, b), jnp.minimum(a, b)


def _bitonic_sort_desc(v):
    n = len(v)
    k = 2
    while k <= n:
        j = k // 2
        while j >= 1:
            for i in range(n):
                l = i ^ j
                if l > i:
                    if (i & k) == 0:
                        _cmpx(v, i, l)
                    else:
                        _cmpx(v, l, i)
            j //= 2
        k *= 2
    return v


def _bitonic_merge_desc(v):
    n = len(v)
    j = n // 2
    while j >= 1:
        for i in range(n):
            l = i ^ j
            if l > i:
                _cmpx(v, i, l)
        j //= 2
    return v


def _max_none(a, b):
    if a is None:
        return b
    if b is None:
        return a
    return jnp.maximum(a, b)


def _merge_top(x, y):
    n = len(x)
    return _bitonic_merge_desc([_max_none(x[i], y[n - 1 - i]) for i in range(n)])


def _top16_rows(s):
    v = [s[r * SUBLANES:(r + 1) * SUBLANES, :] for r in range(PEER_KEYS // SUBLANES)]
    v = _bitonic_sort_desc(v)
    for shift in (4, 2, 1):
        w = [pltpu.roll(x, shift, 0) for x in v]
        v = _merge_top(v, w)
    return v


def _route_kernel(s1_ref, s2_ref, cnt_out, e1_out, rank_out, e2_out):
    k = PEER_TOPK

    def head(hd, carry):
        s1 = s1_ref[hd]
        s2 = s2_ref[hd]
        a = _top16_rows(s1)
        b = _top16_rows(s2)
        cand = {(p, q): a[p] + b[q] for p in range(k) for q in range(k // (p + 1))}
        row0 = [cand[0, q] for q in range(k)]
        rest = [cand[p, q] for p in range(1, k) for q in range(k // (p + 1))]
        l1 = _bitonic_sort_desc(rest[:k])
        l2 = _bitonic_sort_desc(rest[k:2 * k])
        l3 = _bitonic_sort_desc(rest[2 * k:] + [None] * (3 * k - len(rest)))
        top = _merge_top(_merge_top(row0, l1), _merge_top(l2, l3))
        tau = top[k - 1]
        z = jnp.ones_like(tau)
        for t in top[1:]:
            z = z + jnp.exp(t - top[0])
        half_rz = 0.5 / z
        thr = []
        for q in range(k):
            t = jnp.full_like(tau, jnp.inf)
            for p in range(k // (q + 1)):
                t = jnp.where(cand[p, q] >= tau, a[p], t)
            thr.append(t)
        cnts, e1s, ranks, e2s = [], [], [], []
        for r in range(PEER_KEYS // SUBLANES):
            rows = slice(r * SUBLANES, (r + 1) * SUBLANES)
            s1v, s2v = s1[rows, :], s2[rows, :]
            cnt = jnp.zeros_like(s1v)
            rank = jnp.zeros_like(s2v)
            for q in range(k):
                cnt = jnp.where(s1v >= thr[q], q + 1.0, cnt)
                rank = jnp.where(b[q] > s2v, q + 1.0, rank)
            cnts.append(cnt)
            ranks.append(rank)
            e1s.append(jnp.exp(s1v - a[0]))
            e2s.append(jnp.exp(s2v - b[0]) * half_rz)
        cnt_out[hd] = jnp.concatenate(cnts, axis=0)
        e1_out[hd] = jnp.concatenate(e1s, axis=0)
        rank_out[hd] = jnp.concatenate(ranks, axis=0).astype(BF16)
        e2_out[hd] = jnp.concatenate(e2s, axis=0).astype(BF16)
        return carry

    lax.fori_loop(0, PEER_HEADS, head, 0)


def _route(s1t, s2t):
    n_rows = s1t.shape[2]
    tl = ROUTE_TL
    spec = pl.BlockSpec((PEER_HEADS, PEER_KEYS, tl), lambda i: (0, 0, i))
    f32s = jax.ShapeDtypeStruct(s1t.shape, F32)
    bf16s = jax.ShapeDtypeStruct(s1t.shape, BF16)
    return pl.pallas_call(
        _route_kernel,
        grid=(n_rows // tl,),
        in_specs=[spec, spec],
        out_specs=[spec, spec, spec, spec],
        out_shape=[f32s, f32s, bf16s, bf16s],
        compiler_params=_cparams(("parallel",)),
        name="peer_route",
    )(s1t, s2t)


def _expert_kernel(h_ref, u_ref, v_ref, rank_ref, e2_ref, cnt_ref, e1_ref, x_ref, mod_ref, g_ref,
                   o_ref, acc_ref, gel_ref, ga_ref, *, final_norm):
    ei = pl.program_id(1)
    tm = h_ref.shape[0]

    @pl.when(ei == 0)
    def _():
        acc_ref[...] = jnp.zeros_like(acc_ref)

    h = h_ref[...]
    sqrt_half = math.sqrt(0.5)
    zero = jnp.zeros((BF16_ROWS, tm), BF16)
    n_rowgrp = PEER_KEYS // BF16_ROWS
    n_sub = EXP_TE // EXP_SUB

    def activations(sb):
        esl = slice(sb * EXP_SUB, (sb + 1) * EXP_SUB)
        act = lax.dot_general(u_ref[esl, :], h, _NT, preferred_element_type=F32)
        a16 = act.astype(BF16)
        gel_ref[esl, :] = a16 * (1.0 + lax.erf(a16 * sqrt_half))

    def bcast_rows(ref, hd, il):
        parts = [jnp.broadcast_to(ref[hd, il:il + 1, lb * LANES:(lb + 1) * LANES],
                                  (BF16_ROWS, LANES)).astype(BF16) for lb in range(tm // LANES)]
        return jnp.concatenate(parts, axis=1)

    def gated(il):
        acc = [None] * n_rowgrp
        for hd in range(PEER_HEADS):
            cnt = bcast_rows(cnt_ref, hd, il)
            e1 = bcast_rows(e1_ref, hd, il)
            for r in range(n_rowgrp):
                rows = slice(r * BF16_ROWS, (r + 1) * BF16_ROWS)
                term = jnp.where(rank_ref[hd, rows, :] < cnt, e2_ref[hd, rows, :], zero) * e1
                acc[r] = term if acc[r] is None else acc[r] + term
        for r in range(n_rowgrp):
            rows = slice(il * PEER_KEYS + r * BF16_ROWS, il * PEER_KEYS + (r + 1) * BF16_ROWS)
            ga_ref[rows, :] = acc[r] * gel_ref[rows, :]

    def combine(sb):
        esl = slice(sb * EXP_SUB, (sb + 1) * EXP_SUB)
        acc_ref[...] += lax.dot_general(ga_ref[esl, :], v_ref[esl, :], (((0,), (0,)), ((), ())),
                                        preferred_element_type=F32)

    activations(0)
    activations(1)
    gated(0)
    gated(1)
    for sb in range(n_sub):
        combine(sb)
        if sb + 1 < n_sub:
            gated(2 * sb + 2)
            if sb + 2 < n_sub:
                activations(sb + 2)
            gated(2 * sb + 3)

    @pl.when(ei == pl.num_programs(1) - 1)
    def _():
        x2 = x_ref[...] + mod_ref[0, 5] * acc_ref[...]
        if final_norm:
            x2 = _rmsn(x2, g_ref[...])
        o_ref[...] = x2


def _experts(hb, u, v, rank, e2, cnt, e1, x1, mod, gain, n_rows, tiles_per_batch, n_batch,
             final_norm):
    d = x1.shape[1]
    tm, te = EXP_TM, EXP_TE
    n_exp = u.shape[0]
    grp = te // PEER_KEYS

    def tok(t, e):
        return (t, 0)

    def stok(t, e):
        return (0, 0, t)

    def sexp(t, e):
        return (0, e, t)

    return pl.pallas_call(
        functools.partial(_expert_kernel, final_norm=final_norm),
        grid=(n_rows // tm, n_exp // te),
        in_specs=[pl.BlockSpec((tm, d), tok),
                  pl.BlockSpec((te, d), lambda t, e: (e, 0)),
                  pl.BlockSpec((te, d), lambda t, e: (e, 0)),
                  pl.BlockSpec((PEER_HEADS, PEER_KEYS, tm), stok),
                  pl.BlockSpec((PEER_HEADS, PEER_KEYS, tm), stok),
                  pl.BlockSpec((PEER_HEADS, grp, tm), sexp),
                  pl.BlockSpec((PEER_HEADS, grp, tm), sexp),
                  pl.BlockSpec((tm, d), tok),
                  pl.BlockSpec((1, N_MOD, 1, d),
                               lambda t, e: (jnp.minimum(t // tiles_per_batch, n_batch), 0, 0, 0)),
                  pl.BlockSpec((1, d), lambda t, e: (0, 0))],
        out_specs=pl.BlockSpec((tm, d), tok),
        out_shape=jax.ShapeDtypeStruct((n_rows, d), F32),
        scratch_shapes=[pltpu.VMEM((tm, d), F32), pltpu.VMEM((te, tm), BF16),
                        pltpu.VMEM((te, tm), BF16)],
        compiler_params=_cparams(("parallel", "arbitrary")),
        name="peer_experts",
    )(hb, u, v, rank, e2, cnt, e1, x1, mod, gain.reshape(1, d))


def _peer(attn, attn_ctx, xa, mod, w_o, norm_ffn, w_query, k1, k2, u, v, gain_out, n_rows, geom,
          final_norm):
    x1, hb, s1t, s2t = _post_attention(attn, attn_ctx, xa, mod, w_o.astype(BF16), norm_ffn,
                                       w_query.astype(BF16), k1.astype(BF16), k2.astype(BF16),
                                       n_rows, geom)
    cnt, e1, rank, e2 = _route(s1t, s2t)
    return _experts(hb, u.astype(BF16), v.astype(BF16), rank, e2, cnt, e1, x1, mod, gain_out,
                    n_rows, geom[1], geom[2], final_norm)


def _prep_mla(w_in, w_uq, w_ukv):
    d = w_in.shape[0]
    w_in_p = jnp.concatenate([w_in, jnp.zeros((d, LANES - MLA_ROPE), w_in.dtype)], axis=1)
    w_uq_p = jnp.pad(w_uq.reshape(MLA_Q_RANK, MLA_HEADS, MLA_NOPE + MLA_ROPE),
                     ((0, 0), (0, 0), (0, 2 * LANES - MLA_NOPE - MLA_ROPE)))
    w_uq_p = w_uq_p.reshape(MLA_Q_RANK, MLA_HEADS * 2 * LANES)
    w_ukv_p = w_ukv.reshape(MLA_KV_RANK, MLA_HEADS, 2, MLA_NOPE).transpose(0, 2, 1, 3)
    w_ukv_p = w_ukv_p.reshape(MLA_KV_RANK, 2 * MLA_HEADS * MLA_NOPE)
    return w_in_p.astype(BF16), w_uq_p.astype(BF16), w_ukv_p.astype(BF16)


def kernel(x, c, ctx, c_ctx, l0_ada_w, l0_ada_b, l0_norm_mix, l0_norm_ffn, l0_mla_w_in, l0_mla_q_norm, l0_mla_w_uq, l0_mla_kv_norm, l0_mla_w_ukv, l0_mla_w_o, l0_peer_w_query, l0_peer_k1, l0_peer_k2, l0_peer_u, l0_peer_v, l1_ada_w, l1_ada_b, l1_norm_mix, l1_norm_ffn, l1_gqa_w_qkv, l1_gqa_q_norm, l1_gqa_k_norm, l1_gqa_w_o, l1_peer_w_query, l1_peer_k1, l1_peer_k2, l1_peer_u, l1_peer_v, norm_out):
    n_batch, seq, d = x.shape
    n_ctx = ctx.shape[1]
    assert seq % ATTN_TQ == 0 and seq % GRID_W == 0 and (n_batch * seq) % n_ctx == 0
    assert seq % ROW_TILE == 0 and (n_batch * n_ctx) % ROW_TILE == 0
    assert seq % EXP_TM == 0 and (n_batch * n_ctx) % EXP_TM == 0 and ROW_TILE == EXP_TM
    n_lat = n_batch * seq
    n_all = n_lat + n_batch * n_ctx
    geom = (n_lat // ROW_TILE, seq // ROW_TILE, n_batch, seq // ROW_TILE)

    xa = jnp.concatenate([x.reshape(n_lat, d), ctx.reshape(n_batch * n_ctx, d)], axis=0)
    cc = jnp.zeros((SUBLANES, d), F32).at[:n_batch].set(c).at[n_batch].set(c_ctx)
    mod0 = _modulation(cc, l0_ada_w, l0_ada_b).reshape(SUBLANES, N_MOD, 1, d)
    mod1 = _modulation(cc, l1_ada_w, l1_ada_b).reshape(SUBLANES, N_MOD, 1, d)

    w_in_p, w_uq_p, w_ukv_p = _prep_mla(l0_mla_w_in, l0_mla_w_uq, l0_mla_w_ukv)
    q, k, v = _mixer_proj(
        _mla_proj_kernel, xa, l0_norm_mix, mod0,
        [w_in_p, l0_mla_q_norm.reshape(1, -1), w_uq_p, l0_mla_kv_norm.reshape(1, -1), w_ukv_p],
        _rope_tables(seq, MLA_ROPE, ROW_TILE),
        [MLA_HEADS * 2 * LANES, MLA_HEADS * 2 * LANES, MLA_HEADS * 2 * MLA_V], geom, "mla_proj")
    attn, attn_ctx = _attention(q, k, v, n_batch, seq, n_ctx, MLA_HEADS, MLA_HEADS, 2 * LANES,
                                MLA_V, True)
    xa = _peer(attn, attn_ctx, xa, mod0, l0_mla_w_o, l0_norm_ffn, l0_peer_w_query, l0_peer_k1,
               l0_peer_k2, l0_peer_u, l0_peer_v, norm_out, n_all, geom, False)

    q, k, v = _mixer_proj(
        _gqa_proj_kernel, xa, l1_norm_mix, mod1,
        [l1_gqa_w_qkv.astype(BF16), l1_gqa_q_norm.reshape(1, -1), l1_gqa_k_norm.reshape(1, -1)],
        _rope_tables(seq, GQA_DIM, ROW_TILE),
        [GQA_HEADS * GQA_DIM, GQA_KV_HEADS * GQA_DIM, GQA_KV_HEADS * 2 * GQA_DIM], geom, "gqa_proj")
    attn, _ = _attention(q, k, v, n_batch, seq, n_ctx, GQA_HEADS, GQA_KV_HEADS, GQA_DIM, GQA_DIM,
                         False)
    out = _peer(attn, None, xa, mod1, l1_gqa_w_o, l1_norm_ffn, l1_peer_w_query, l1_peer_k1,
                l1_peer_k2, l1_peer_u, l1_peer_v, norm_out, n_lat, geom, True)
    return out.reshape(n_batch, seq, d)
```

```python
import functools
import math

import jax
import jax.numpy as jnp
from jax import lax
from jax.experimental import pallas as pl
from jax.experimental.pallas import tpu as pltpu

F32 = jnp.float32
BF16 = jnp.bfloat16

GRID_W = 64
NORM_EPS = 1e-6
ROPE_THETA = 10000.0
N_MOD = 6
MLA_HEADS = 8
MLA_Q_RANK = 384
MLA_KV_RANK = 256
MLA_NOPE = 128
MLA_ROPE = 64
MLA_V = 128
GQA_HEADS = 8
GQA_KV_HEADS = 2
GQA_DIM = 128
PEER_HEADS = 8
PEER_KEYS = 128
PEER_HALF = 128
PEER_TOPK = 16

LANES = 128
SUBLANES = 8
BF16_ROWS = 16
MXU_DIM = 256
VMEM_LIMIT = 56 * 1024 * 1024

ROW_TILE = 512
ATTN_TQ = 2048
ATTN_CHAIN = 256
ATTN_SLOTS = 4
ROUTE_TL = 256
EXP_TM = 512
EXP_TE = 2048
EXP_SUB = MXU_DIM

_NT = (((1,), (1,)), ((), ()))
LOG2E = math.log2(math.e)


def _cparams(sem):
    return pltpu.CompilerParams(dimension_semantics=sem, vmem_limit_bytes=VMEM_LIMIT)


def _rmsn(x, g):
    ms = jnp.mean(x * x, axis=-1, keepdims=True)
    return x * lax.rsqrt(ms + NORM_EPS) * g


def _mod_kernel(c_ref, w_ref, b_ref, o_ref):
    s = jax.nn.silu(c_ref[...])
    o_ref[...] = jnp.dot(s, w_ref[...], preferred_element_type=F32,
                         precision=lax.Precision.HIGHEST) + b_ref[...]


def _modulation(cc, w, b):
    d, n = w.shape
    bn = n // 4
    return pl.pallas_call(
        _mod_kernel,
        grid=(n // bn,),
        in_specs=[pl.BlockSpec((SUBLANES, d), lambda j: (0, 0)),
                  pl.BlockSpec((d, bn), lambda j: (0, j)),
                  pl.BlockSpec((1, bn), lambda j: (0, j))],
        out_specs=pl.BlockSpec((SUBLANES, bn), lambda j: (0, j)),
        out_shape=jax.ShapeDtypeStruct((SUBLANES, n), F32),
        compiler_params=_cparams(("arbitrary",)),
        name="adaln_mod",
    )(cc, w, b.reshape(1, n))


def _rope_tables(seq, rot_dim, extra_rows):
    rows = seq // GRID_W
    row = jnp.repeat(jnp.arange(rows, dtype=F32), GRID_W)
    col = jnp.tile(jnp.arange(GRID_W, dtype=F32), rows)
    d_axis = rot_dim // 2
    inv_freq = ROPE_THETA ** (-jnp.arange(0, d_axis, 2, dtype=F32) / d_axis)
    ang_r = row[:, None] * inv_freq
    ang_c = col[:, None] * inv_freq
    ang = jnp.concatenate([ang_r, ang_r, ang_c, ang_c], axis=-1)
    cos, sin = jnp.cos(ang), jnp.sin(ang)
    pad = LANES - rot_dim
    cos = jnp.pad(cos, ((0, extra_rows), (0, pad)), constant_values=1.0)
    sin = jnp.pad(sin, ((0, extra_rows), (0, pad)))
    quarter = rot_dim // 4
    first = (jnp.arange(LANES) % (2 * quarter)) < quarter
    sin_a = jnp.where(first[None, :], -sin, 0.0)
    sin_b = jnp.where(first[None, :], 0.0, sin)
    return cos, sin_a, sin_b


def _rope(c, cos, sin_a, sin_b, quarter):
    return (c * cos + pltpu.roll(c, LANES - quarter, 1) * sin_a
            + pltpu.roll(c, quarter, 1) * sin_b)


def _mla_proj_kernel(x_ref, g_ref, mod_ref, win_ref, qn_ref, wuq_ref, kvn_ref, wukv_ref,
                     cos_ref, sa_ref, sb_ref, q_out, k_out, v_out):
    h = _rmsn(x_ref[...], g_ref[...])
    h = h * (1.0 + mod_ref[0, 1]) + mod_ref[0, 0]
    a = jnp.dot(h.astype(BF16), win_ref[...], preferred_element_type=F32)
    cq = _rmsn(a[:, :MLA_Q_RANK], qn_ref[...]).astype(BF16)
    ckv = _rmsn(a[:, MLA_Q_RANK:MLA_Q_RANK + MLA_KV_RANK], kvn_ref[...]).astype(BF16)
    cos, sa, sb = cos_ref[...], sa_ref[...], sb_ref[...]
    quarter = MLA_ROPE // 4
    kr = _rope(a[:, MLA_Q_RANK + MLA_KV_RANK:], cos, sa, sb, quarter).astype(BF16)
    q = jnp.dot(cq, wuq_ref[...], preferred_element_type=F32)
    kv = jnp.dot(ckv, wukv_ref[...], preferred_element_type=F32)
    scale = LOG2E / math.sqrt(MLA_NOPE + MLA_ROPE)
    hw = 2 * LANES
    ones = jnp.ones((q.shape[0], LANES), BF16)
    for hd in range(MLA_HEADS):
        q_out[:, hd * hw:hd * hw + LANES] = (q[:, hd * hw:hd * hw + LANES] * scale).astype(BF16)
        qr = _rope(q[:, hd * hw + LANES:(hd + 1) * hw], cos, sa, sb, quarter) * scale
        q_out[:, hd * hw + LANES:(hd + 1) * hw] = qr.astype(BF16)
        k_out[:, hd * hw:hd * hw + LANES] = kv[:, hd * LANES:(hd + 1) * LANES].astype(BF16)
        k_out[:, hd * hw + LANES:(hd + 1) * hw] = kr
        v0 = (MLA_HEADS + hd) * LANES
        v_out[:, hd * hw:hd * hw + LANES] = kv[:, v0:v0 + LANES].astype(BF16)
        v_out[:, hd * hw + LANES:(hd + 1) * hw] = ones


def _gqa_proj_kernel(x_ref, g_ref, mod_ref, wqkv_ref, qn_ref, kn_ref,
                     cos_ref, sa_ref, sb_ref, q_out, k_out, v_out):
    h = _rmsn(x_ref[...], g_ref[...])
    h = h * (1.0 + mod_ref[0, 1]) + mod_ref[0, 0]
    qkv = jnp.dot(h.astype(BF16), wqkv_ref[...], preferred_element_type=F32)
    cos, sa, sb = cos_ref[...], sa_ref[...], sb_ref[...]
    quarter = GQA_DIM // 4
    scale = LOG2E / math.sqrt(GQA_DIM)
    for hd in range(GQA_HEADS):
        c = _rmsn(qkv[:, hd * LANES:(hd + 1) * LANES], qn_ref[...])
        q_out[:, hd * LANES:(hd + 1) * LANES] = (_rope(c, cos, sa, sb, quarter) * scale).astype(BF16)
    k0 = GQA_HEADS * LANES
    for hd in range(GQA_KV_HEADS):
        c = _rmsn(qkv[:, k0 + hd * LANES:k0 + (hd + 1) * LANES], kn_ref[...])
        k_out[:, hd * LANES:(hd + 1) * LANES] = _rope(c, cos, sa, sb, quarter).astype(BF16)
    v0 = k0 + GQA_KV_HEADS * LANES
    ones = jnp.ones((qkv.shape[0], LANES), BF16)
    for hd in range(GQA_KV_HEADS):
        v_out[:, 2 * hd * LANES:(2 * hd + 1) * LANES] = (
            qkv[:, v0 + hd * LANES:v0 + (hd + 1) * LANES].astype(BF16))
        v_out[:, (2 * hd + 1) * LANES:(2 * hd + 2) * LANES] = ones


def _row_maps(n_lat_tiles, tiles_per_batch, n_batch, tiles_per_seq):
    def row(i):
        return (i, 0)

    def const(i):
        return (0, 0)

    def mod(i):
        return (jnp.minimum(i // tiles_per_batch, n_batch), 0, 0, 0)

    def rope(i):
        return (jnp.where(i < n_lat_tiles, i % tiles_per_seq, tiles_per_seq), 0)

    return row, const, mod, rope


def _mixer_proj(kern, xa, gain, mod, weights, tables, out_widths, geom, name):
    n_rows, d = xa.shape
    tm = ROW_TILE
    row, const, modmap, ropemap = _row_maps(*geom)
    in_specs = [pl.BlockSpec((tm, d), row),
                pl.BlockSpec((1, d), const),
                pl.BlockSpec((1, N_MOD, 1, d), modmap)]
    in_specs += [pl.BlockSpec(w.shape, const) for w in weights]
    in_specs += [pl.BlockSpec((tm, LANES), ropemap) for _ in tables]
    return pl.pallas_call(
        kern,
        grid=(n_rows // tm,),
        in_specs=in_specs,
        out_specs=[pl.BlockSpec((tm, w), row) for w in out_widths],
        out_shape=[jax.ShapeDtypeStruct((n_rows, w), BF16) for w in out_widths],
        compiler_params=_cparams(("parallel",)),
        name=name,
    )(xa, gain.reshape(1, d), mod, *weights, *tables)


def _attn_kernel(q_ref, kl_ref, kc_ref, vl_ref, vc_ref, o_ref, s_ref, p_ref):
    n_lat = kl_ref.shape[0]
    dv = o_ref.shape[1]
    for c in range(ATTN_TQ // ATTN_CHAIN):
        rows = slice(c * ATTN_CHAIN, (c + 1) * ATTN_CHAIN)
        slot = c % ATTN_SLOTS
        q = q_ref[rows, :]
        s_ref[slot, :, :n_lat] = lax.dot_general(q, kl_ref[...], _NT, preferred_element_type=F32)
        s_ref[slot, :, n_lat:] = lax.dot_general(q, kc_ref[...], _NT, preferred_element_type=F32)
        s = s_ref[slot]
        p_ref[slot] = jnp.exp2(s - s.max(axis=-1, keepdims=True)).astype(BF16)
        o = (jnp.dot(p_ref[slot, :, :n_lat], vl_ref[...], preferred_element_type=F32)
             + jnp.dot(p_ref[slot, :, n_lat:], vc_ref[...], preferred_element_type=F32))
        o_ref[rows, :] = (o[:, :dv] / o[:, dv:]).astype(BF16)


def _attn_ctx_kernel(q_ref, kc_ref, vc_ref, o_ref):
    dv = o_ref.shape[1]
    s = lax.dot_general(q_ref[...], kc_ref[...], _NT, preferred_element_type=F32)
    p = jnp.exp2(s - s.max(axis=-1, keepdims=True)).astype(BF16)
    o = jnp.dot(p, vc_ref[...], preferred_element_type=F32)
    o_ref[...] = (o[:, :dv] / o[:, dv:]).astype(BF16)


def _attention(q, k, v, n_batch, seq, ctx, n_heads, n_kv_heads, dqk, dv, with_ctx_queries):
    tq = ATTN_TQ
    nq = seq // tq
    group = n_heads // n_kv_heads
    ctx_block0 = n_batch * seq // ctx
    dve = 2 * dv

    def qmap(b, h, i):
        return (b * nq + i, h)

    def klat(b, h, i):
        return (b, h // group)

    def kctx(b, h, i):
        return (ctx_block0 + b, h // group)

    out = pl.pallas_call(
        _attn_kernel,
        grid=(n_batch, n_heads, nq),
        in_specs=[pl.BlockSpec((tq, dqk), qmap),
                  pl.BlockSpec((seq, dqk), klat),
                  pl.BlockSpec((ctx, dqk), kctx),
                  pl.BlockSpec((seq, dve), klat),
                  pl.BlockSpec((ctx, dve), kctx)],
        out_specs=pl.BlockSpec((tq, dv), qmap),
        out_shape=jax.ShapeDtypeStruct((n_batch * seq, n_heads * dv), BF16),
        scratch_shapes=[pltpu.VMEM((ATTN_SLOTS, ATTN_CHAIN, seq + ctx), F32),
                        pltpu.VMEM((ATTN_SLOTS, ATTN_CHAIN, seq + ctx), BF16)],
        compiler_params=_cparams(("parallel", "parallel", "arbitrary")),
        name="attention",
    )(q, k, k, v, v)
    if not with_ctx_queries:
        return out, None

    out_ctx = pl.pallas_call(
        _attn_ctx_kernel,
        grid=(n_batch, n_heads),
        in_specs=[pl.BlockSpec((ctx, dqk), lambda b, h: (ctx_block0 + b, h)),
                  pl.BlockSpec((ctx, dqk), lambda b, h: (ctx_block0 + b, h // group)),
                  pl.BlockSpec((ctx, dve), lambda b, h: (ctx_block0 + b, h // group))],
        out_specs=pl.BlockSpec((ctx, dv), lambda b, h: (b, h)),
        out_shape=jax.ShapeDtypeStruct((n_batch * ctx, n_heads * dv), BF16),
        compiler_params=_cparams(("parallel", "parallel")),
        name="attention_ctx",
    )(q, k, v)
    return out, out_ctx


def _post_kernel(*refs, n_lat_tiles, has_ctx):
    attn = refs[0][...]
    if has_ctx:
        attn = jnp.where(pl.program_id(0) >= n_lat_tiles, refs[1][...], attn)
    (x_ref, mod_ref, wo_ref, g_ref, wq_ref, k1_ref, k2_ref,
     x1_out, h_out, cnt_out, e1_out, rank_out, e2_out, s1_out, s2_out) = refs[2 if has_ctx else 1:]
    y = jnp.dot(attn, wo_ref[...], preferred_element_type=F32)
    x1 = x_ref[...] + mod_ref[0, 2] * y
    x1_out[...] = x1
    h = _rmsn(x1, g_ref[...])
    hb = (h * (1.0 + mod_ref[0, 4]) + mod_ref[0, 3]).astype(BF16)
    h_out[...] = hb
    qp = jnp.dot(hb, wq_ref[...], preferred_element_type=F32).astype(BF16)
    hw = 2 * PEER_HALF
    for hd in range(PEER_HEADS):
        s1_out[hd] = lax.dot_general(k1_ref[hd], qp[:, hd * hw:hd * hw + PEER_HALF], _NT,
                                     preferred_element_type=F32)
        s2_out[hd] = lax.dot_general(k2_ref[hd], qp[:, hd * hw + PEER_HALF:(hd + 1) * hw], _NT,
                                     preferred_element_type=F32)
    for lb in range(s1_out.shape[2] // ROUTE_TL):
        _route_kernel(s1_out, s2_out, cnt_out, e1_out, rank_out, e2_out,
                      lanes=slice(lb * ROUTE_TL, (lb + 1) * ROUTE_TL))


def _post_attention(attn, attn_ctx, xa, mod, w_o, gain, w_query, k1, k2, n_rows, geom):
    d = xa.shape[1]
    tm = ROW_TILE
    row, const, modmap, _ = _row_maps(*geom)
    n_lat_tiles = geom[0]
    kern = functools.partial(_post_kernel, n_lat_tiles=n_lat_tiles, has_ctx=attn_ctx is not None)
    attn_specs = [pl.BlockSpec((tm, attn.shape[1]), lambda i: (jnp.minimum(i, n_lat_tiles - 1), 0))]
    attn_args = [attn]
    if attn_ctx is not None:
        attn_specs.append(pl.BlockSpec((tm, attn.shape[1]),
                                       lambda i: (jnp.maximum(i - n_lat_tiles, 0), 0)))
        attn_args.append(attn_ctx)

    def const3(i):
        return (0, 0, 0)

    def smap(i):
        return (0, 0, i)

    sdims = (PEER_HEADS, PEER_KEYS, n_rows)
    return pl.pallas_call(
        kern,
        grid=(n_rows // tm,),
        in_specs=attn_specs + [
            pl.BlockSpec((tm, d), row),
            pl.BlockSpec((1, N_MOD, 1, d), modmap),
            pl.BlockSpec(w_o.shape, const),
            pl.BlockSpec((1, d), const),
            pl.BlockSpec(w_query.shape, const),
            pl.BlockSpec(k1.shape, const3),
            pl.BlockSpec(k2.shape, const3)],
        out_specs=[pl.BlockSpec((tm, d), row),
                   pl.BlockSpec((tm, d), row),
                   pl.BlockSpec((PEER_HEADS, PEER_KEYS, tm), smap),
                   pl.BlockSpec((PEER_HEADS, PEER_KEYS, tm), smap),
                   pl.BlockSpec((PEER_HEADS, PEER_KEYS, tm), smap),
                   pl.BlockSpec((PEER_HEADS, PEER_KEYS, tm), smap)],
        out_shape=[jax.ShapeDtypeStruct((n_rows, d), F32),
                   jax.ShapeDtypeStruct((n_rows, d), BF16),
                   jax.ShapeDtypeStruct(sdims, F32), jax.ShapeDtypeStruct(sdims, F32),
                   jax.ShapeDtypeStruct(sdims, BF16), jax.ShapeDtypeStruct(sdims, BF16)],
        scratch_shapes=[pltpu.VMEM((PEER_HEADS, PEER_KEYS, tm), F32),
                        pltpu.VMEM((PEER_HEADS, PEER_KEYS, tm), F32)],
        compiler_params=_cparams(("parallel",)),
        name="post_attention",
    )(*attn_args, xa, mod, w_o, gain.reshape(1, d), w_query, k1, k2)


def _cmpx(v, i, j):
    a, b = v[i], v[j]
    if b is None:
        return
    if a is None:
        v[i], v[j] = b, None
        return
    v[i], v[j] = jnp.maximum(a, b), jnp.minimum(a, b)


def _bitonic_sort_desc(v):
    n = len(v)
    k = 2
    while k <= n:
        j = k // 2
        while j >= 1:
            for i in range(n):
                l = i ^ j
                if l > i:
                    if (i & k) == 0:
                        _cmpx(v, i, l)
                    else:
                        _cmpx(v, l, i)
            j //= 2
        k *= 2
    return v


def _bitonic_merge_desc(v):
    n = len(v)
    j = n // 2
    while j >= 1:
        for i in range(n):
            l = i ^ j
            if l > i:
                _cmpx(v, i, l)
        j //= 2
    return v


def _max_none(a, b):
    if a is None:
        return b
    if b is None:
        return a
    return jnp.maximum(a, b)


def _merge_top(x, y):
    n = len(x)
    return _bitonic_merge_desc([_max_none(x[i], y[n - 1 - i]) for i in range(n)])


def _top16_rows(s):
    v = [s[r * SUBLANES:(r + 1) * SUBLANES, :] for r in range(PEER_KEYS // SUBLANES)]
    v = _bitonic_sort_desc(v)
    for shift in (4, 2, 1):
        w = [pltpu.roll(x, shift, 0) for x in v]
        v = _merge_top(v, w)
    return v


def _route_kernel(s1_ref, s2_ref, cnt_out, e1_out, rank_out, e2_out, lanes=slice(None)):
    k = PEER_TOPK

    def head(hd, carry):
        s1 = s1_ref[hd, :, lanes]
        s2 = s2_ref[hd, :, lanes]
        a = _top16_rows(s1)
        b = _top16_rows(s2)
        cand = {(p, q): a[p] + b[q] for p in range(k) for q in range(k // (p + 1))}
        row0 = [cand[0, q] for q in range(k)]
        rest = [cand[p, q] for p in range(1, k) for q in range(k // (p + 1))]
        l1 = _bitonic_sort_desc(rest[:k])
        l2 = _bitonic_sort_desc(rest[k:2 * k])
        l3 = _bitonic_sort_desc(rest[2 * k:] + [None] * (3 * k - len(rest)))
        top = _merge_top(_merge_top(row0, l1), _merge_top(l2, l3))
        tau = top[k - 1]
        z = jnp.ones_like(tau)
        for t in top[1:]:
            z = z + jnp.exp(t - top[0])
        half_rz = 0.5 / z
        thr = []
        for q in range(k):
            t = jnp.full_like(tau, jnp.inf)
            for p in range(k // (q + 1)):
                t = jnp.where(cand[p, q] >= tau, a[p], t)
            thr.append(t)
        cnts, e1s, ranks, e2s = [], [], [], []
        for r in range(PEER_KEYS // SUBLANES):
            rows = slice(r * SUBLANES, (r + 1) * SUBLANES)
            s1v, s2v = s1[rows, :], s2[rows, :]
            cnt = jnp.zeros_like(s1v)
            rank = jnp.zeros_like(s2v)
            for q in range(k):
                cnt = jnp.where(s1v >= thr[q], q + 1.0, cnt)
                rank = jnp.where(b[q] > s2v, q + 1.0, rank)
            cnts.append(cnt)
            ranks.append(rank)
            e1s.append(jnp.exp(s1v - a[0]))
            e2s.append(jnp.exp(s2v - b[0]) * half_rz)
        cnt_out[hd, :, lanes] = jnp.concatenate(cnts, axis=0)
        e1_out[hd, :, lanes] = jnp.concatenate(e1s, axis=0)
        rank_out[hd, :, lanes] = jnp.concatenate(ranks, axis=0).astype(BF16)
        e2_out[hd, :, lanes] = jnp.concatenate(e2s, axis=0).astype(BF16)
        return carry

    lax.fori_loop(0, PEER_HEADS, head, 0)


def _route(s1t, s2t):
    n_rows = s1t.shape[2]
    tl = ROUTE_TL
    spec = pl.BlockSpec((PEER_HEADS, PEER_KEYS, tl), lambda i: (0, 0, i))
    f32s = jax.ShapeDtypeStruct(s1t.shape, F32)
    bf16s = jax.ShapeDtypeStruct(s1t.shape, BF16)
    return pl.pallas_call(
        _route_kernel,
        grid=(n_rows // tl,),
        in_specs=[spec, spec],
        out_specs=[spec, spec, spec, spec],
        out_shape=[f32s, f32s, bf16s, bf16s],
        compiler_params=_cparams(("parallel",)),
        name="peer_route",
    )(s1t, s2t)


def _expert_kernel(h_ref, u_ref, v_ref, rank_ref, e2_ref, cnt_ref, e1_ref, x_ref, mod_ref, g_ref,
                   o_ref, acc_ref, gel_ref, ga_ref, *, final_norm):
    ei = pl.program_id(1)
    tm = h_ref.shape[0]

    @pl.when(ei == 0)
    def _():
        acc_ref[...] = jnp.zeros_like(acc_ref)

    h = h_ref[...]
    sqrt_half = math.sqrt(0.5)
    zero = jnp.zeros((BF16_ROWS, tm), BF16)
    n_rowgrp = PEER_KEYS // BF16_ROWS
    n_sub = EXP_TE // EXP_SUB

    def activations(sb):
        esl = slice(sb * EXP_SUB, (sb + 1) * EXP_SUB)
        act = lax.dot_general(u_ref[esl, :], h, _NT, preferred_element_type=F32)
        a16 = act.astype(BF16)
        gel_ref[esl, :] = a16 * (1.0 + lax.erf(a16 * sqrt_half))

    def bcast_rows(ref, hd, il):
        parts = [jnp.broadcast_to(ref[hd, il:il + 1, lb * LANES:(lb + 1) * LANES],
                                  (BF16_ROWS, LANES)).astype(BF16) for lb in range(tm // LANES)]
        return jnp.concatenate(parts, axis=1)

    def gated(il):
        acc = [None] * n_rowgrp
        for hd in range(PEER_HEADS):
            cnt = bcast_rows(cnt_ref, hd, il)
            e1 = bcast_rows(e1_ref, hd, il)
            for r in range(n_rowgrp):
                rows = slice(r * BF16_ROWS, (r + 1) * BF16_ROWS)
                term = jnp.where(rank_ref[hd, rows, :] < cnt, e2_ref[hd, rows, :], zero) * e1
                acc[r] = term if acc[r] is None else acc[r] + term
        for r in range(n_rowgrp):
            rows = slice(il * PEER_KEYS + r * BF16_ROWS, il * PEER_KEYS + (r + 1) * BF16_ROWS)
            ga_ref[rows, :] = acc[r] * gel_ref[rows, :]

    def combine(sb):
        esl = slice(sb * EXP_SUB, (sb + 1) * EXP_SUB)
        acc_ref[...] += lax.dot_general(ga_ref[esl, :], v_ref[esl, :], (((0,), (0,)), ((), ())),
                                        preferred_element_type=F32)

    activations(0)
    activations(1)
    gated(0)
    gated(1)
    for sb in range(n_sub):
        combine(sb)
        if sb + 1 < n_sub:
            gated(2 * sb + 2)
            if sb + 2 < n_sub:
                activations(sb + 2)
            gated(2 * sb + 3)

    @pl.when(ei == pl.num_programs(1) - 1)
    def _():
        x2 = x_ref[...] + mod_ref[0, 5] * acc_ref[...]
        if final_norm:
            x2 = _rmsn(x2, g_ref[...])
        o_ref[...] = x2


def _experts(hb, u, v, rank, e2, cnt, e1, x1, mod, gain, n_rows, tiles_per_batch, n_batch,
             final_norm):
    d = x1.shape[1]
    tm, te = EXP_TM, EXP_TE
    n_exp = u.shape[0]
    grp = te // PEER_KEYS

    def tok(t, e):
        return (t, 0)

    def stok(t, e):
        return (0, 0, t)

    def sexp(t, e):
        return (0, e, t)

    return pl.pallas_call(
        functools.partial(_expert_kernel, final_norm=final_norm),
        grid=(n_rows // tm, n_exp // te),
        in_specs=[pl.BlockSpec((tm, d), tok),
                  pl.BlockSpec((te, d), lambda t, e: (e, 0)),
                  pl.BlockSpec((te, d), lambda t, e: (e, 0)),
                  pl.BlockSpec((PEER_HEADS, PEER_KEYS, tm), stok),
                  pl.BlockSpec((PEER_HEADS, PEER_KEYS, tm), stok),
                  pl.BlockSpec((PEER_HEADS, grp, tm), sexp),
                  pl.BlockSpec((PEER_HEADS, grp, tm), sexp),
                  pl.BlockSpec((tm, d), tok),
                  pl.BlockSpec((1, N_MOD, 1, d),
                               lambda t, e: (jnp.minimum(t // tiles_per_batch, n_batch), 0, 0, 0)),
                  pl.BlockSpec((1, d), lambda t, e: (0, 0))],
        out_specs=pl.BlockSpec((tm, d), tok),
        out_shape=jax.ShapeDtypeStruct((n_rows, d), F32),
        scratch_shapes=[pltpu.VMEM((tm, d), F32), pltpu.VMEM((te, tm), BF16),
                        pltpu.VMEM((te, tm), BF16)],
        compiler_params=_cparams(("parallel", "arbitrary")),
        name="peer_experts",
    )(hb, u, v, rank, e2, cnt, e1, x1, mod, gain.reshape(1, d))


def _peer(attn, attn_ctx, xa, mod, w_o, norm_ffn, w_query, k1, k2, u, v, gain_out, n_rows, geom,
          final_norm):
    x1, hb, cnt, e1, rank, e2 = _post_attention(attn, attn_ctx, xa, mod, w_o.astype(BF16), norm_ffn,
                                                w_query.astype(BF16), k1.astype(BF16),
                                                k2.astype(BF16), n_rows, geom)
    return _experts(hb, u.astype(BF16), v.astype(BF16), rank, e2, cnt, e1, x1, mod, gain_out,
                    n_rows, geom[1], geom[2], final_norm)


def _prep_mla(w_in, w_uq, w_ukv):
    d = w_in.shape[0]
    w_in_p = jnp.concatenate([w_in, jnp.zeros((d, LANES - MLA_ROPE), w_in.dtype)], axis=1)
    w_uq_p = jnp.pad(w_uq.reshape(MLA_Q_RANK, MLA_HEADS, MLA_NOPE + MLA_ROPE),
                     ((0, 0), (0, 0), (0, 2 * LANES - MLA_NOPE - MLA_ROPE)))
    w_uq_p = w_uq_p.reshape(MLA_Q_RANK, MLA_HEADS * 2 * LANES)
    w_ukv_p = w_ukv.reshape(MLA_KV_RANK, MLA_HEADS, 2, MLA_NOPE).transpose(0, 2, 1, 3)
    w_ukv_p = w_ukv_p.reshape(MLA_KV_RANK, 2 * MLA_HEADS * MLA_NOPE)
    return w_in_p.astype(BF16), w_uq_p.astype(BF16), w_ukv_p.astype(BF16)


def kernel(x, c, ctx, c_ctx, l0_ada_w, l0_ada_b, l0_norm_mix, l0_norm_ffn, l0_mla_w_in, l0_mla_q_norm, l0_mla_w_uq, l0_mla_kv_norm, l0_mla_w_ukv, l0_mla_w_o, l0_peer_w_query, l0_peer_k1, l0_peer_k2, l0_peer_u, l0_peer_v, l1_ada_w, l1_ada_b, l1_norm_mix, l1_norm_ffn, l1_gqa_w_qkv, l1_gqa_q_norm, l1_gqa_k_norm, l1_gqa_w_o, l1_peer_w_query, l1_peer_k1, l1_peer_k2, l1_peer_u, l1_peer_v, norm_out):
    n_batch, seq, d = x.shape
    n_ctx = ctx.shape[1]
    assert seq % ATTN_TQ == 0 and seq % GRID_W == 0 and (n_batch * seq) % n_ctx == 0
    assert seq % ROW_TILE == 0 and (n_batch * n_ctx) % ROW_TILE == 0
    assert seq % EXP_TM == 0 and (n_batch * n_ctx) % EXP_TM == 0 and ROW_TILE == EXP_TM
    n_lat = n_batch * seq
    n_all = n_lat + n_batch * n_ctx
    geom = (n_lat // ROW_TILE, seq // ROW_TILE, n_batch, seq // ROW_TILE)

    xa = jnp.concatenate([x.reshape(n_lat, d), ctx.reshape(n_batch * n_ctx, d)], axis=0)
    cc = jnp.zeros((SUBLANES, d), F32).at[:n_batch].set(c).at[n_batch].set(c_ctx)
    mod0 = _modulation(cc, l0_ada_w, l0_ada_b).reshape(SUBLANES, N_MOD, 1, d)
    mod1 = _modulation(cc, l1_ada_w, l1_ada_b).reshape(SUBLANES, N_MOD, 1, d)

    w_in_p, w_uq_p, w_ukv_p = _prep_mla(l0_mla_w_in, l0_mla_w_uq, l0_mla_w_ukv)
    q, k, v = _mixer_proj(
        _mla_proj_kernel, xa, l0_norm_mix, mod0,
        [w_in_p, l0_mla_q_norm.reshape(1, -1), w_uq_p, l0_mla_kv_norm.reshape(1, -1), w_ukv_p],
        _rope_tables(seq, MLA_ROPE, ROW_TILE),
        [MLA_HEADS * 2 * LANES, MLA_HEADS * 2 * LANES, MLA_HEADS * 2 * MLA_V], geom, "mla_proj")
    attn, attn_ctx = _attention(q, k, v, n_batch, seq, n_ctx, MLA_HEADS, MLA_HEADS, 2 * LANES,
                                MLA_V, True)
    xa = _peer(attn, attn_ctx, xa, mod0, l0_mla_w_o, l0_norm_ffn, l0_peer_w_query, l0_peer_k1,
               l0_peer_k2, l0_peer_u, l0_peer_v, norm_out, n_all, geom, False)

    q, k, v = _mixer_proj(
        _gqa_proj_kernel, xa, l1_norm_mix, mod1,
        [l1_gqa_w_qkv.astype(BF16), l1_gqa_q_norm.reshape(1, -1), l1_gqa_k_norm.reshape(1, -1)],
        _rope_tables(seq, GQA_DIM, ROW_TILE),
        [GQA_HEADS * GQA_DIM, GQA_KV_HEADS * GQA_DIM, GQA_KV_HEADS * 2 * GQA_DIM], geom, "gqa_proj")
    attn, _ = _attention(q, k, v, n_batch, seq, n_ctx, GQA_HEADS, GQA_KV_HEADS, GQA_DIM, GQA_DIM,
                         False)
    out = _peer(attn, None, xa, mod1, l1_gqa_w_o, l1_norm_ffn, l1_peer_w_query, l1_peer_k1,
                l1_peer_k2, l1_peer_u, l1_peer_v, norm_out, n_lat, geom, True)
    return out.reshape(n_batch, seq, d)
```

```python
import functools
import math

import jax
import jax.numpy as jnp
from jax import lax
from jax.experimental import pallas as pl
from jax.experimental.pallas import tpu as pltpu

F32 = jnp.float32
BF16 = jnp.bfloat16

GRID_W = 64
NORM_EPS = 1e-6
ROPE_THETA = 10000.0
N_MOD = 6
MLA_HEADS = 8
MLA_Q_RANK = 384
MLA_KV_RANK = 256
MLA_NOPE = 128
MLA_ROPE = 64
MLA_V = 128
GQA_HEADS = 8
GQA_KV_HEADS = 2
GQA_DIM = 128
PEER_HEADS = 8
PEER_KEYS = 128
PEER_HALF = 128
PEER_TOPK = 16

LANES = 128
SUBLANES = 8
BF16_ROWS = 16
MXU_DIM = 256
VMEM_LIMIT = 56 * 1024 * 1024

ROW_TILE = 512
ATTN_TQ = 2048
ATTN_CHAIN = 256
ATTN_SLOTS = 4
ROUTE_TL = 256
EXP_TM = 512
EXP_TE = 2048
EXP_SUB = MXU_DIM

_NT = (((1,), (1,)), ((), ()))
LOG2E = math.log2(math.e)


def _cparams(sem):
    return pltpu.CompilerParams(dimension_semantics=sem, vmem_limit_bytes=VMEM_LIMIT)


def _rmsn(x, g):
    ms = jnp.mean(x * x, axis=-1, keepdims=True)
    return x * lax.rsqrt(ms + NORM_EPS) * g


def _mod_kernel(c_ref, w_ref, b_ref, o_ref):
    s = jax.nn.silu(c_ref[...])
    o_ref[...] = jnp.dot(s, w_ref[...], preferred_element_type=F32,
                         precision=lax.Precision.HIGHEST) + b_ref[...]


def _modulation(cc, w, b):
    d, n = w.shape
    bn = n // 4
    return pl.pallas_call(
        _mod_kernel,
        grid=(n // bn,),
        in_specs=[pl.BlockSpec((SUBLANES, d), lambda j: (0, 0)),
                  pl.BlockSpec((d, bn), lambda j: (0, j)),
                  pl.BlockSpec((1, bn), lambda j: (0, j))],
        out_specs=pl.BlockSpec((SUBLANES, bn), lambda j: (0, j)),
        out_shape=jax.ShapeDtypeStruct((SUBLANES, n), F32),
        compiler_params=_cparams(("arbitrary",)),
        name="adaln_mod",
    )(cc, w, b.reshape(1, n))


def _rope_tables(seq, rot_dim, extra_rows):
    rows = seq // GRID_W
    row = jnp.repeat(jnp.arange(rows, dtype=F32), GRID_W)
    col = jnp.tile(jnp.arange(GRID_W, dtype=F32), rows)
    d_axis = rot_dim // 2
    inv_freq = ROPE_THETA ** (-jnp.arange(0, d_axis, 2, dtype=F32) / d_axis)
    ang_r = row[:, None] * inv_freq
    ang_c = col[:, None] * inv_freq
    ang = jnp.concatenate([ang_r, ang_r, ang_c, ang_c], axis=-1)
    cos, sin = jnp.cos(ang), jnp.sin(ang)
    pad = LANES - rot_dim
    cos = jnp.pad(cos, ((0, extra_rows), (0, pad)), constant_values=1.0)
    sin = jnp.pad(sin, ((0, extra_rows), (0, pad)))
    quarter = rot_dim // 4
    first = (jnp.arange(LANES) % (2 * quarter)) < quarter
    sin_a = jnp.where(first[None, :], -sin, 0.0)
    sin_b = jnp.where(first[None, :], 0.0, sin)
    return cos, sin_a, sin_b


def _rope(c, cos, sin_a, sin_b, quarter):
    return (c * cos + pltpu.roll(c, LANES - quarter, 1) * sin_a
            + pltpu.roll(c, quarter, 1) * sin_b)


def _mla_proj_kernel(x_ref, g_ref, mod_ref, win_ref, qn_ref, wuq_ref, kvn_ref, wukv_ref,
                     cos_ref, sa_ref, sb_ref, q_out, k_out, v_out):
    h = _rmsn(x_ref[...], g_ref[...])
    h = h * (1.0 + mod_ref[0, 1]) + mod_ref[0, 0]
    a = jnp.dot(h.astype(BF16), win_ref[...], preferred_element_type=F32)
    cq = _rmsn(a[:, :MLA_Q_RANK], qn_ref[...]).astype(BF16)
    ckv = _rmsn(a[:, MLA_Q_RANK:MLA_Q_RANK + MLA_KV_RANK], kvn_ref[...]).astype(BF16)
    cos, sa, sb = cos_ref[...], sa_ref[...], sb_ref[...]
    quarter = MLA_ROPE // 4
    kr = _rope(a[:, MLA_Q_RANK + MLA_KV_RANK:], cos, sa, sb, quarter).astype(BF16)
    q = jnp.dot(cq, wuq_ref[...], preferred_element_type=F32)
    kv = jnp.dot(ckv, wukv_ref[...], preferred_element_type=F32)
    scale = LOG2E / math.sqrt(MLA_NOPE + MLA_ROPE)
    hw = 2 * LANES
    ones = jnp.ones((q.shape[0], LANES), BF16)
    for hd in range(MLA_HEADS):
        q_out[:, hd * hw:hd * hw + LANES] = (q[:, hd * hw:hd * hw + LANES] * scale).astype(BF16)
        qr = _rope(q[:, hd * hw + LANES:(hd + 1) * hw], cos, sa, sb, quarter) * scale
        q_out[:, hd * hw + LANES:(hd + 1) * hw] = qr.astype(BF16)
        k_out[:, hd * hw:hd * hw + LANES] = kv[:, hd * LANES:(hd + 1) * LANES].astype(BF16)
        k_out[:, hd * hw + LANES:(hd + 1) * hw] = kr
        v0 = (MLA_HEADS + hd) * LANES
        v_out[:, hd * hw:hd * hw + LANES] = kv[:, v0:v0 + LANES].astype(BF16)
        v_out[:, hd * hw + LANES:(hd + 1) * hw] = ones


def _gqa_proj_kernel(x_ref, g_ref, mod_ref, wqkv_ref, qn_ref, kn_ref,
                     cos_ref, sa_ref, sb_ref, q_out, k_out, v_out):
    h = _rmsn(x_ref[...], g_ref[...])
    h = h * (1.0 + mod_ref[0, 1]) + mod_ref[0, 0]
    qkv = jnp.dot(h.astype(BF16), wqkv_ref[...], preferred_element_type=F32)
    cos, sa, sb = cos_ref[...], sa_ref[...], sb_ref[...]
    quarter = GQA_DIM // 4
    scale = LOG2E / math.sqrt(GQA_DIM)
    for hd in range(GQA_HEADS):
        c = _rmsn(qkv[:, hd * LANES:(hd + 1) * LANES], qn_ref[...])
        q_out[:, hd * LANES:(hd + 1) * LANES] = (_rope(c, cos, sa, sb, quarter) * scale).astype(BF16)
    k0 = GQA_HEADS * LANES
    for hd in range(GQA_KV_HEADS):
        c = _rmsn(qkv[:, k0 + hd * LANES:k0 + (hd + 1) * LANES], kn_ref[...])
        k_out[:, hd * LANES:(hd + 1) * LANES] = _rope(c, cos, sa, sb, quarter).astype(BF16)
    v0 = k0 + GQA_KV_HEADS * LANES
    ones = jnp.ones((qkv.shape[0], LANES), BF16)
    for hd in range(GQA_KV_HEADS):
        v_out[:, 2 * hd * LANES:(2 * hd + 1) * LANES] = (
            qkv[:, v0 + hd * LANES:v0 + (hd + 1) * LANES].astype(BF16))
        v_out[:, (2 * hd + 1) * LANES:(2 * hd + 2) * LANES] = ones


def _row_maps(n_lat_tiles, tiles_per_batch, n_batch, tiles_per_seq):
    def row(i):
        return (i, 0)

    def const(i):
        return (0, 0)

    def mod(i):
        return (jnp.minimum(i // tiles_per_batch, n_batch), 0, 0, 0)

    def rope(i):
        return (jnp.where(i < n_lat_tiles, i % tiles_per_seq, tiles_per_seq), 0)

    return row, const, mod, rope


def _mixer_proj(kern, xa, gain, mod, weights, tables, out_widths, geom, name):
    n_rows, d = xa.shape
    tm = ROW_TILE
    row, const, modmap, ropemap = _row_maps(*geom)
    in_specs = [pl.BlockSpec((tm, d), row),
                pl.BlockSpec((1, d), const),
                pl.BlockSpec((1, N_MOD, 1, d), modmap)]
    in_specs += [pl.BlockSpec(w.shape, const) for w in weights]
    in_specs += [pl.BlockSpec((tm, LANES), ropemap) for _ in tables]
    return pl.pallas_call(
        kern,
        grid=(n_rows // tm,),
        in_specs=in_specs,
        out_specs=[pl.BlockSpec((tm, w), row) for w in out_widths],
        out_shape=[jax.ShapeDtypeStruct((n_rows, w), BF16) for w in out_widths],
        compiler_params=_cparams(("parallel",)),
        name=name,
    )(xa, gain.reshape(1, d), mod, *weights, *tables)


def _attn_kernel(q_ref, kl_ref, kc_ref, vl_ref, vc_ref, o_ref, s_ref, p_ref):
    n_lat = kl_ref.shape[0]
    dv = o_ref.shape[1]
    for c in range(ATTN_TQ // ATTN_CHAIN):
        rows = slice(c * ATTN_CHAIN, (c + 1) * ATTN_CHAIN)
        slot = c % ATTN_SLOTS
        q = q_ref[rows, :]
        s_ref[slot, :, :n_lat] = lax.dot_general(q, kl_ref[...], _NT, preferred_element_type=F32)
        s_ref[slot, :, n_lat:] = lax.dot_general(q, kc_ref[...], _NT, preferred_element_type=F32)
        s = s_ref[slot]
        p_ref[slot] = jnp.exp2(s - s.max(axis=-1, keepdims=True)).astype(BF16)
        o = (jnp.dot(p_ref[slot, :, :n_lat], vl_ref[...], preferred_element_type=F32)
             + jnp.dot(p_ref[slot, :, n_lat:], vc_ref[...], preferred_element_type=F32))
        o_ref[rows, :] = (o[:, :dv] / o[:, dv:]).astype(BF16)


def _attn_ctx_kernel(q_ref, kc_ref, vc_ref, o_ref):
    dv = o_ref.shape[1]
    s = lax.dot_general(q_ref[...], kc_ref[...], _NT, preferred_element_type=F32)
    p = jnp.exp2(s - s.max(axis=-1, keepdims=True)).astype(BF16)
    o = jnp.dot(p, vc_ref[...], preferred_element_type=F32)
    o_ref[...] = (o[:, :dv] / o[:, dv:]).astype(BF16)


def _attention(q, k, v, n_batch, seq, ctx, n_heads, n_kv_heads, dqk, dv, with_ctx_queries):
    tq = ATTN_TQ
    nq = seq // tq
    group = n_heads // n_kv_heads
    ctx_block0 = n_batch * seq // ctx
    dve = 2 * dv

    def qmap(b, h, i):
        return (b * nq + i, h)

    def klat(b, h, i):
        return (b, h // group)

    def kctx(b, h, i):
        return (ctx_block0 + b, h // group)

    out = pl.pallas_call(
        _attn_kernel,
        grid=(n_batch, n_heads, nq),
        in_specs=[pl.BlockSpec((tq, dqk), qmap),
                  pl.BlockSpec((seq, dqk), klat),
                  pl.BlockSpec((ctx, dqk), kctx),
                  pl.BlockSpec((seq, dve), klat),
                  pl.BlockSpec((ctx, dve), kctx)],
        out_specs=pl.BlockSpec((tq, dv), qmap),
        out_shape=jax.ShapeDtypeStruct((n_batch * seq, n_heads * dv), BF16),
        scratch_shapes=[pltpu.VMEM((ATTN_SLOTS, ATTN_CHAIN, seq + ctx), F32),
                        pltpu.VMEM((ATTN_SLOTS, ATTN_CHAIN, seq + ctx), BF16)],
        compiler_params=_cparams(("parallel", "parallel", "arbitrary")),
        name="attention",
    )(q, k, k, v, v)
    if not with_ctx_queries:
        return out, None

    out_ctx = pl.pallas_call(
        _attn_ctx_kernel,
        grid=(n_batch, n_heads),
        in_specs=[pl.BlockSpec((ctx, dqk), lambda b, h: (ctx_block0 + b, h)),
                  pl.BlockSpec((ctx, dqk), lambda b, h: (ctx_block0 + b, h // group)),
                  pl.BlockSpec((ctx, dve), lambda b, h: (ctx_block0 + b, h // group))],
        out_specs=pl.BlockSpec((ctx, dv), lambda b, h: (b, h)),
        out_shape=jax.ShapeDtypeStruct((n_batch * ctx, n_heads * dv), BF16),
        compiler_params=_cparams(("parallel", "parallel")),
        name="attention_ctx",
    )(q, k, v)
    return out, out_ctx


def _post_kernel(*refs, n_lat_tiles, has_ctx):
    attn = refs[0][...]
    if has_ctx:
        attn = jnp.where(pl.program_id(0) >= n_lat_tiles, refs[1][...], attn)
    (x_ref, mod_ref, wo_ref, g_ref, wq_ref, k1_ref, k2_ref,
     x1_out, h_out, s1_out, s2_out) = refs[2 if has_ctx else 1:]
    y = jnp.dot(attn, wo_ref[...], preferred_element_type=F32)
    x1 = x_ref[...] + mod_ref[0, 2] * y
    x1_out[...] = x1
    h = _rmsn(x1, g_ref[...])
    hb = (h * (1.0 + mod_ref[0, 4]) + mod_ref[0, 3]).astype(BF16)
    h_out[...] = hb
    qp = jnp.dot(hb, wq_ref[...], preferred_element_type=F32).astype(BF16)
    hw = 2 * PEER_HALF
    for hd in range(PEER_HEADS):
        s1_out[hd] = lax.dot_general(k1_ref[hd], qp[:, hd * hw:hd * hw + PEER_HALF], _NT,
                                     preferred_element_type=F32)
        s2_out[hd] = lax.dot_general(k2_ref[hd], qp[:, hd * hw + PEER_HALF:(hd + 1) * hw], _NT,
                                     preferred_element_type=F32)


def _post_attention(attn, attn_ctx, xa, mod, w_o, gain, w_query, k1, k2, n_rows, geom):
    d = xa.shape[1]
    tm = ROW_TILE
    row, const, modmap, _ = _row_maps(*geom)
    n_lat_tiles = geom[0]
    kern = functools.partial(_post_kernel, n_lat_tiles=n_lat_tiles, has_ctx=attn_ctx is not None)
    attn_specs = [pl.BlockSpec((tm, attn.shape[1]), lambda i: (jnp.minimum(i, n_lat_tiles - 1), 0))]
    attn_args = [attn]
    if attn_ctx is not None:
        attn_specs.append(pl.BlockSpec((tm, attn.shape[1]),
                                       lambda i: (jnp.maximum(i - n_lat_tiles, 0), 0)))
        attn_args.append(attn_ctx)

    def const3(i):
        return (0, 0, 0)

    def smap(i):
        return (0, 0, i)

    sshape = jax.ShapeDtypeStruct((PEER_HEADS, PEER_KEYS, n_rows), F32)
    return pl.pallas_call(
        kern,
        grid=(n_rows // tm,),
        in_specs=attn_specs + [
            pl.BlockSpec((tm, d), row),
            pl.BlockSpec((1, N_MOD, 1, d), modmap),
            pl.BlockSpec(w_o.shape, const),
            pl.BlockSpec((1, d), const),
            pl.BlockSpec(w_query.shape, const),
            pl.BlockSpec(k1.shape, const3),
            pl.BlockSpec(k2.shape, const3)],
        out_specs=[pl.BlockSpec((tm, d), row),
                   pl.BlockSpec((tm, d), row),
                   pl.BlockSpec((PEER_HEADS, PEER_KEYS, tm), smap),
                   pl.BlockSpec((PEER_HEADS, PEER_KEYS, tm), smap)],
        out_shape=[jax.ShapeDtypeStruct((n_rows, d), F32),
                   jax.ShapeDtypeStruct((n_rows, d), BF16),
                   sshape, sshape],
        compiler_params=_cparams(("parallel",)),
        name="post_attention",
    )(*attn_args, xa, mod, w_o, gain.reshape(1, d), w_query, k1, k2)


def _cmpx(v, i, j):
    a, b = v[i], v[j]
    if b is None:
        return
    if a is None:
        v[i], v[j] = b, None
        return
    v[i], v[j] = jnp.maximum(a, b), jnp.minimum(a, b)


def _bitonic_sort_desc(v):
    n = len(v)
    k = 2
    while k <= n:
        j = k // 2
        while j >= 1:
            for i in range(n):
                l = i ^ j
                if l > i:
                    if (i & k) == 0:
                        _cmpx(v, i, l)
                    else:
                        _cmpx(v, l, i)
            j //= 2
        k *= 2
    return v


def _bitonic_merge_desc(v):
    n = len(v)
    j = n // 2
    while j >= 1:
        for i in range(n):
            l = i ^ j
            if l > i:
                _cmpx(v, i, l)
        j //= 2
    return v


def _max_none(a, b):
    if a is None:
        return b
    if b is None:
        return a
    return jnp.maximum(a, b)


def _merge_top(x, y):
    n = len(x)
    return _bitonic_merge_desc([_max_none(x[i], y[n - 1 - i]) for i in range(n)])


def _top16_rows(s):
    v = [s[r * SUBLANES:(r + 1) * SUBLANES, :] for r in range(PEER_KEYS // SUBLANES)]
    v = _bitonic_sort_desc(v)
    for shift in (4, 2, 1):
        w = [pltpu.roll(x, shift, 0) for x in v]
        v = _merge_top(v, w)
    return v


def _route_kernel(s1_ref, s2_ref, cnt_out, e1_out, rank_out, e2_out):
    k = PEER_TOPK

    def head(hd, carry):
        s1 = s1_ref[hd]
        s2 = s2_ref[hd]
        a = _top16_rows(s1)
        b = _top16_rows(s2)
        cand = {(p, q): a[p] + b[q] for p in range(k) for q in range(k // (p + 1))}
        row0 = [cand[0, q] for q in range(k)]
        rest = [cand[p, q] for p in range(1, k) for q in range(k // (p + 1))]
        l1 = _bitonic_sort_desc(rest[:k])
        l2 = _bitonic_sort_desc(rest[k:2 * k])
        l3 = _bitonic_sort_desc(rest[2 * k:] + [None] * (3 * k - len(rest)))
        top = _merge_top(_merge_top(row0, l1), _merge_top(l2, l3))
        tau = top[k - 1]
        z = jnp.ones_like(tau)
        for t in top[1:]:
            z = z + jnp.exp(t - top[0])
        half_rz = 0.5 / z
        thr = []
        for q in range(k):
            t = jnp.full_like(tau, jnp.inf)
            for p in range(k // (q + 1)):
                t = jnp.where(cand[p, q] >= tau, a[p], t)
            thr.append(t)
        cnts, e1s, ranks, e2s = [], [], [], []
        for r in range(PEER_KEYS // SUBLANES):
            rows = slice(r * SUBLANES, (r + 1) * SUBLANES)
            s1v, s2v = s1[rows, :], s2[rows, :]
            cnt = jnp.zeros_like(s1v)
            rank = jnp.zeros_like(s2v)
            for q in range(k):
                cnt = jnp.where(s1v >= thr[q], q + 1.0, cnt)
                rank = jnp.where(b[q] > s2v, q + 1.0, rank)
            cnts.append(cnt)
            ranks.append(rank)
            e1s.append(jnp.exp(s1v - a[0]))
            e2s.append(jnp.exp(s2v - b[0]) * half_rz)
        cnt_out[hd] = jnp.concatenate(cnts, axis=0)
        e1_out[hd] = jnp.concatenate(e1s, axis=0)
        rank_out[hd] = jnp.concatenate(ranks, axis=0).astype(BF16)
        e2_out[hd] = jnp.concatenate(e2s, axis=0).astype(BF16)
        return carry

    lax.fori_loop(0, PEER_HEADS, head, 0)


def _route(s1t, s2t):
    n_rows = s1t.shape[2]
    tl = ROUTE_TL
    spec = pl.BlockSpec((PEER_HEADS, PEER_KEYS, tl), lambda i: (0, 0, i))
    f32s = jax.ShapeDtypeStruct(s1t.shape, F32)
    bf16s = jax.ShapeDtypeStruct(s1t.shape, BF16)
    return pl.pallas_call(
        _route_kernel,
        grid=(n_rows // tl,),
        in_specs=[spec, spec],
        out_specs=[spec, spec, spec, spec],
        out_shape=[f32s, f32s, bf16s, bf16s],
        compiler_params=_cparams(("parallel",)),
        name="peer_route",
    )(s1t, s2t)


def _expert_kernel(h_ref, u_ref, v_ref, rank_ref, e2_ref, cnt_ref, e1_ref, x_ref, mod_ref, g_ref,
                   o_ref, acc_ref, gel_ref, ga_ref, *, final_norm):
    ei = pl.program_id(1)
    tm = h_ref.shape[0]

    @pl.when(ei == 0)
    def _():
        acc_ref[...] = jnp.zeros_like(acc_ref)

    h = h_ref[...]
    sqrt_half = math.sqrt(0.5)
    zero = jnp.zeros((BF16_ROWS, tm), BF16)
    n_rowgrp = PEER_KEYS // BF16_ROWS
    n_sub = EXP_TE // EXP_SUB

    def activations(sb):
        esl = slice(sb * EXP_SUB, (sb + 1) * EXP_SUB)
        act = lax.dot_general(u_ref[esl, :].astype(BF16), h, _NT,
                              preferred_element_type=F32)
        a16 = act.astype(BF16)
        gel_ref[esl, :] = a16 * (1.0 + lax.erf(a16 * sqrt_half))

    def bcast_rows(ref, hd, il):
        parts = [jnp.broadcast_to(ref[hd, il:il + 1, lb * LANES:(lb + 1) * LANES],
                                  (BF16_ROWS, LANES)).astype(BF16) for lb in range(tm // LANES)]
        return jnp.concatenate(parts, axis=1)

    def gated(il):
        acc = [None] * n_rowgrp
        for hd in range(PEER_HEADS):
            cnt = bcast_rows(cnt_ref, hd, il)
            e1 = bcast_rows(e1_ref, hd, il)
            for r in range(n_rowgrp):
                rows = slice(r * BF16_ROWS, (r + 1) * BF16_ROWS)
                term = jnp.where(rank_ref[hd, rows, :] < cnt, e2_ref[hd, rows, :], zero) * e1
                acc[r] = term if acc[r] is None else acc[r] + term
        for r in range(n_rowgrp):
            rows = slice(il * PEER_KEYS + r * BF16_ROWS, il * PEER_KEYS + (r + 1) * BF16_ROWS)
            ga_ref[rows, :] = acc[r] * gel_ref[rows, :]

    def combine(sb):
        esl = slice(sb * EXP_SUB, (sb + 1) * EXP_SUB)
        acc_ref[...] += lax.dot_general(ga_ref[esl, :], v_ref[esl, :], (((0,), (0,)), ((), ())),
                                        preferred_element_type=F32)

    activations(0)
    activations(1)
    gated(0)
    gated(1)
    for sb in range(n_sub):
        combine(sb)
        if sb + 1 < n_sub:
            gated(2 * sb + 2)
            if sb + 2 < n_sub:
                activations(sb + 2)
            gated(2 * sb + 3)

    @pl.when(ei == pl.num_programs(1) - 1)
    def _():
        x2 = x_ref[...] + mod_ref[0, 5] * acc_ref[...]
        if final_norm:
            x2 = _rmsn(x2, g_ref[...])
        o_ref[...] = x2


def _experts(hb, u, v, rank, e2, cnt, e1, x1, mod, gain, n_rows, tiles_per_batch, n_batch,
             final_norm):
    d = x1.shape[1]
    tm, te = EXP_TM, EXP_TE
    n_exp = u.shape[0]
    grp = te // PEER_KEYS

    def tok(t, e):
        return (t, 0)

    def stok(t, e):
        return (0, 0, t)

    def sexp(t, e):
        return (0, e, t)

    return pl.pallas_call(
        functools.partial(_expert_kernel, final_norm=final_norm),
        grid=(n_rows // tm, n_exp // te),
        in_specs=[pl.BlockSpec((tm, d), tok),
                  pl.BlockSpec((te, d), lambda t, e: (e, 0)),
                  pl.BlockSpec((te, d), lambda t, e: (e, 0)),
                  pl.BlockSpec((PEER_HEADS, PEER_KEYS, tm), stok),
                  pl.BlockSpec((PEER_HEADS, PEER_KEYS, tm), stok),
                  pl.BlockSpec((PEER_HEADS, grp, tm), sexp),
                  pl.BlockSpec((PEER_HEADS, grp, tm), sexp),
                  pl.BlockSpec((tm, d), tok),
                  pl.BlockSpec((1, N_MOD, 1, d),
                               lambda t, e: (jnp.minimum(t // tiles_per_batch, n_batch), 0, 0, 0)),
                  pl.BlockSpec((1, d), lambda t, e: (0, 0))],
        out_specs=pl.BlockSpec((tm, d), tok),
        out_shape=jax.ShapeDtypeStruct((n_rows, d), F32),
        scratch_shapes=[pltpu.VMEM((tm, d), F32), pltpu.VMEM((te, tm), BF16),
                        pltpu.VMEM((te, tm), BF16)],
        compiler_params=_cparams(("parallel", "arbitrary")),
        name="peer_experts",
    )(hb, u, v, rank, e2, cnt, e1, x1, mod, gain.reshape(1, d))


def _peer(attn, attn_ctx, xa, mod, w_o, norm_ffn, w_query, k1, k2, u, v, gain_out, n_rows, geom,
          final_norm):
    x1, hb, s1t, s2t = _post_attention(attn, attn_ctx, xa, mod, w_o.astype(BF16), norm_ffn,
                                       w_query.astype(BF16), k1.astype(BF16), k2.astype(BF16),
                                       n_rows, geom)
    cnt, e1, rank, e2 = _route(s1t, s2t)
    return _experts(hb, u, v.astype(BF16), rank, e2, cnt, e1, x1, mod, gain_out,
                    n_rows, geom[1], geom[2], final_norm)


def _prep_mla(w_in, w_uq, w_ukv):
    d = w_in.shape[0]
    w_in_p = jnp.concatenate([w_in, jnp.zeros((d, LANES - MLA_ROPE), w_in.dtype)], axis=1)
    w_uq_p = jnp.pad(w_uq.reshape(MLA_Q_RANK, MLA_HEADS, MLA_NOPE + MLA_ROPE),
                     ((0, 0), (0, 0), (0, 2 * LANES - MLA_NOPE - MLA_ROPE)))
    w_uq_p = w_uq_p.reshape(MLA_Q_RANK, MLA_HEADS * 2 * LANES)
    w_ukv_p = w_ukv.reshape(MLA_KV_RANK, MLA_HEADS, 2, MLA_NOPE).transpose(0, 2, 1, 3)
    w_ukv_p = w_ukv_p.reshape(MLA_KV_RANK, 2 * MLA_HEADS * MLA_NOPE)
    return w_in_p.astype(BF16), w_uq_p.astype(BF16), w_ukv_p.astype(BF16)


def kernel(x, c, ctx, c_ctx, l0_ada_w, l0_ada_b, l0_norm_mix, l0_norm_ffn, l0_mla_w_in, l0_mla_q_norm, l0_mla_w_uq, l0_mla_kv_norm, l0_mla_w_ukv, l0_mla_w_o, l0_peer_w_query, l0_peer_k1, l0_peer_k2, l0_peer_u, l0_peer_v, l1_ada_w, l1_ada_b, l1_norm_mix, l1_norm_ffn, l1_gqa_w_qkv, l1_gqa_q_norm, l1_gqa_k_norm, l1_gqa_w_o, l1_peer_w_query, l1_peer_k1, l1_peer_k2, l1_peer_u, l1_peer_v, norm_out):
    n_batch, seq, d = x.shape
    n_ctx = ctx.shape[1]
    assert seq % ATTN_TQ == 0 and seq % GRID_W == 0 and (n_batch * seq) % n_ctx == 0
    assert seq % ROW_TILE == 0 and (n_batch * n_ctx) % ROW_TILE == 0
    assert seq % EXP_TM == 0 and (n_batch * n_ctx) % EXP_TM == 0 and ROW_TILE == EXP_TM
    n_lat = n_batch * seq
    n_all = n_lat + n_batch * n_ctx
    geom = (n_lat // ROW_TILE, seq // ROW_TILE, n_batch, seq // ROW_TILE)

    xa = jnp.concatenate([x.reshape(n_lat, d), ctx.reshape(n_batch * n_ctx, d)], axis=0)
    cc = jnp.zeros((SUBLANES, d), F32).at[:n_batch].set(c).at[n_batch].set(c_ctx)
    mod0 = _modulation(cc, l0_ada_w, l0_ada_b).reshape(SUBLANES, N_MOD, 1, d)
    mod1 = _modulation(cc, l1_ada_w, l1_ada_b).reshape(SUBLANES, N_MOD, 1, d)

    w_in_p, w_uq_p, w_ukv_p = _prep_mla(l0_mla_w_in, l0_mla_w_uq, l0_mla_w_ukv)
    q, k, v = _mixer_proj(
        _mla_proj_kernel, xa, l0_norm_mix, mod0,
        [w_in_p, l0_mla_q_norm.reshape(1, -1), w_uq_p, l0_mla_kv_norm.reshape(1, -1), w_ukv_p],
        _rope_tables(seq, MLA_ROPE, ROW_TILE),
        [MLA_HEADS * 2 * LANES, MLA_HEADS * 2 * LANES, MLA_HEADS * 2 * MLA_V], geom, "mla_proj")
    attn, attn_ctx = _attention(q, k, v, n_batch, seq, n_ctx, MLA_HEADS, MLA_HEADS, 2 * LANES,
                                MLA_V, True)
    xa = _peer(attn, attn_ctx, xa, mod0, l0_mla_w_o, l0_norm_ffn, l0_peer_w_query, l0_peer_k1,
               l0_peer_k2, l0_peer_u, l0_peer_v, norm_out, n_all, geom, False)

    q, k, v = _mixer_proj(
        _gqa_proj_kernel, xa, l1_norm_mix, mod1,
        [l1_gqa_w_qkv.astype(BF16), l1_gqa_q_norm.reshape(1, -1), l1_gqa_k_norm.reshape(1, -1)],
        _rope_tables(seq, GQA_DIM, ROW_TILE),
        [GQA_HEADS * GQA_DIM, GQA_KV_HEADS * GQA_DIM, GQA_KV_HEADS * 2 * GQA_DIM], geom, "gqa_proj")
    attn, _ = _attention(q, k, v, n_batch, seq, n_ctx, GQA_HEADS, GQA_KV_HEADS, GQA_DIM, GQA_DIM,
                         False)
    out = _peer(attn, None, xa, mod1, l1_gqa_w_o, l1_norm_ffn, l1_peer_w_query, l1_peer_k1,
                l1_peer_k2, l1_peer_u, l1_peer_v, norm_out, n_lat, geom, True)
    return out.reshape(n_batch, seq, d)
```
